```python
import jax, jax.numpy as jnp
from jax import lax
import numpy as np

D_MODEL = 1024
BATCH = 8
SEQ = 2048
DEPTH = 4

N_MIXERS = 3
MEM_LEN = 256
EPS = 1e-6
ROPE_THETA = 10000.0
MAX_POS_OFFSET = 4096

MLA_HEADS = 8
MLA_NOPE = 128
MLA_ROPE = 64
MLA_V = 128
MLA_Q_RANK = 384
MLA_KV_RANK = 256
Q_BLOCK = 128

GDN_HEADS = 8
GDN_DK = 128
GDN_DV = 128
GDN_CONV = 4
GDN_CHUNK = 64
GDN_QKV = GDN_HEADS * (2 * GDN_DK + GDN_DV)
GDN_PROJ = GDN_QKV + GDN_HEADS * GDN_DV + 2 * GDN_HEADS

SC_WIDTH = D_MODEL
SC_CONV = 3

X_HEADS = 4
X_HEAD_DIM = D_MODEL // X_HEADS

D_FF = 4 * D_MODEL

N_A = (DEPTH + 2) // N_MIXERS
N_B = (DEPTH + 1) // N_MIXERS
N_C = DEPTH // N_MIXERS

kernel_name = "hybrid_mla_gdn_shortconv_memxattn"


def rms_norm(x, g):
    xf = x.astype(jnp.float32)
    y = xf * lax.rsqrt(jnp.mean(xf * xf, axis=-1, keepdims=True) + EPS)
    return (y * g.astype(jnp.float32)).astype(x.dtype)


def rope_tables(positions):
    inv_freq = ROPE_THETA ** (-jnp.arange(0, MLA_ROPE, 2, dtype=jnp.float32) / MLA_ROPE)
    ang = positions.astype(jnp.float32)[..., None] * inv_freq
    return jnp.cos(ang), jnp.sin(ang)


def apply_rope(x, cos, sin):
    c = cos[:, :, None, :]
    s = sin[:, :, None, :]
    x1, x2 = jnp.split(x.astype(jnp.float32), 2, axis=-1)
    return jnp.concatenate([x1 * c - x2 * s, x2 * c + x1 * s], axis=-1).astype(x.dtype)


def causal_depthwise_conv(x, w):
    k, c = w.shape
    return lax.conv_general_dilated(
        x, w[:, None, :].astype(x.dtype), window_strides=(1,), padding=[(k - 1, 0)],
        dimension_numbers=("NWC", "WIO", "NWC"), feature_group_count=c)


def mla_mixer(h, cos, sin, w_in, q_norm, kv_norm, w_uq, w_ukv, w_o):
    b, s, _ = h.shape
    z = h @ w_in
    c_q, c_kv, k_rope = jnp.split(z, [MLA_Q_RANK, MLA_Q_RANK + MLA_KV_RANK], axis=-1)
    q = (rms_norm(c_q, q_norm) @ w_uq).reshape(b, s, MLA_HEADS, MLA_NOPE + MLA_ROPE)
    q_nope = q[..., :MLA_NOPE]
    q_rope = apply_rope(q[..., MLA_NOPE:], cos, sin)
    kv = (rms_norm(c_kv, kv_norm) @ w_ukv).reshape(b, s, MLA_HEADS, MLA_NOPE + MLA_V)
    k_nope, v = kv[..., :MLA_NOPE], kv[..., MLA_NOPE:]
    k_rope = apply_rope(k_rope[:, :, None, :], cos, sin)[:, :, 0, :]
    scale = (MLA_NOPE + MLA_ROPE) ** -0.5
    outs = []
    for start in range(0, s, Q_BLOCK):
        end = start + Q_BLOCK
        sc = (jnp.einsum("bqhd,bkhd->bhqk", q_nope[:, start:end], k_nope[:, :end])
              + jnp.einsum("bqhr,bkr->bhqk", q_rope[:, start:end], k_rope[:, :end]))
        sc = sc.astype(jnp.float32) * scale
        mask = (start + jnp.arange(Q_BLOCK))[:, None] >= jnp.arange(end)[None, :]
        sc = jnp.where(mask, sc, -jnp.inf)
        p = jax.nn.softmax(sc, axis=-1).astype(v.dtype)
        outs.append(jnp.einsum("bhqk,bkhd->bqhd", p, v[:, :end]))
    o = jnp.concatenate(outs, axis=1).reshape(b, s, MLA_HEADS * MLA_V)
    return o @ w_o


def chunk_gated_delta_rule(q, k, v, g, beta):
    b, s, h, dk = q.shape
    dv = v.shape[-1]
    c = GDN_CHUNK
    n = s // c

    def to_chunks(t):
        t = t.astype(jnp.float32).reshape((b, n, c, h) + t.shape[3:])
        return jnp.moveaxis(t, (1, 3), (0, 2))

    qc, kc, vc = to_chunks(q), to_chunks(k), to_chunks(v)
    gc = lax.cumsum(to_chunks(g), axis=3)
    bc = to_chunks(beta)
    tri = jnp.tril(jnp.ones((c, c), dtype=bool))
    strict = jnp.tril(jnp.ones((c, c), dtype=bool), -1)
    decay = jnp.exp(jnp.where(tri, gc[..., :, None] - gc[..., None, :], -jnp.inf))
    k_beta = kc * bc[..., None]
    m = jnp.where(strict, jnp.einsum("nbhid,nbhjd->nbhij", k_beta, kc) * decay, 0.0)
    eye = jnp.eye(c, dtype=jnp.float32)
    t_inv = lax.linalg.triangular_solve(eye + m, jnp.broadcast_to(eye, m.shape),
                                        left_side=True, lower=True, unit_diagonal=True)
    u = t_inv @ (vc * bc[..., None])
    w = t_inv @ (k_beta * jnp.exp(gc)[..., None])
    attn_intra = jnp.einsum("nbhid,nbhjd->nbhij", qc, kc) * decay

    def step(state, xs):
        q_i, k_i, u_i, w_i, g_i, a_i = xs
        v_new = u_i - w_i @ state
        o_i = (q_i * jnp.exp(g_i)[..., None]) @ state + a_i @ v_new
        g_last = g_i[..., -1:]
        state = (state * jnp.exp(g_last)[..., None]
                 + jnp.einsum("bhcd,bhce->bhde", k_i * jnp.exp(g_last - g_i)[..., None], v_new))
        return state, o_i

    s0 = jnp.zeros((b, h, dk, dv), jnp.float32)
    _, o = lax.scan(step, s0, (qc, kc, u, w, gc, attn_intra))
    return jnp.moveaxis(o, (0, 2), (1, 3)).reshape(b, s, h, dv)


def gdn_mixer(h, w_in, conv_w, a_log, dt_bias, o_norm, w_o):
    b, s, _ = h.shape
    z = h @ w_in
    qkv, gate, beta_logit, a_logit = jnp.split(
        z, [GDN_QKV, GDN_QKV + GDN_HEADS * GDN_DV, GDN_QKV + GDN_HEADS * GDN_DV + GDN_HEADS], axis=-1)
    qkv = jax.nn.silu(causal_depthwise_conv(qkv, conv_w))
    q, k, v = jnp.split(qkv, [GDN_HEADS * GDN_DK, 2 * GDN_HEADS * GDN_DK], axis=-1)
    q = q.reshape(b, s, GDN_HEADS, GDN_DK).astype(jnp.float32)
    k = k.reshape(b, s, GDN_HEADS, GDN_DK).astype(jnp.float32)
    v = v.reshape(b, s, GDN_HEADS, GDN_DV)
    q = q * lax.rsqrt(jnp.sum(q * q, -1, keepdims=True) + EPS) * (GDN_DK ** -0.5)
    k = k * lax.rsqrt(jnp.sum(k * k, -1, keepdims=True) + EPS)
    beta = jax.nn.sigmoid(beta_logit.astype(jnp.float32))
    g = -jnp.exp(a_log.astype(jnp.float32)) * jax.nn.softplus(
        a_logit.astype(jnp.float32) + dt_bias.astype(jnp.float32))
    o = chunk_gated_delta_rule(q, k, v, g, beta)
    o = rms_norm(o, o_norm) * jax.nn.silu(gate.reshape(b, s, GDN_HEADS, GDN_DV).astype(jnp.float32))
    return o.reshape(b, s, GDN_HEADS * GDN_DV).astype(h.dtype) @ w_o


def short_conv_mixer(h, w_in, conv_w, w_o):
    z = h @ w_in
    b_gate, c_gate, u = jnp.split(z, 3, axis=-1)
    y = b_gate * causal_depthwise_conv(c_gate * u, conv_w)
    return y @ w_o


def memory_cross_attention(h, mem_n, w_q, w_kv, w_o):
    b, s, _ = h.shape
    m = mem_n.shape[1]
    q = (h @ w_q).reshape(b, s, X_HEADS, X_HEAD_DIM)
    k, v = jnp.split(mem_n @ w_kv, 2, axis=-1)
    k = k.reshape(b, m, X_HEADS, X_HEAD_DIM)
    v = v.reshape(b, m, X_HEADS, X_HEAD_DIM)
    sc = jnp.einsum("bqhd,bkhd->bhqk", q, k).astype(jnp.float32) * (X_HEAD_DIM ** -0.5)
    p = jax.nn.softmax(sc, axis=-1).astype(v.dtype)
    o = jnp.einsum("bhqk,bkhd->bqhd", p, v).reshape(b, s, X_HEADS * X_HEAD_DIM)
    return o @ w_o


def relu2_mlp(h, w1, w2):
    return jnp.square(jax.nn.relu(h @ w1)) @ w2


def setup_inputs(seed: int = 0) -> dict:
    key = jax.random.key(seed)
    ks = iter(jax.random.split(key, 40))

    def w(shape, fan_in):
        return jax.random.normal(next(ks), shape, jnp.float32) * (fan_in ** -0.5)

    def gain(shape):
        return 1.0 + 0.02 * jax.random.normal(next(ks), shape, jnp.float32)

    x = jax.random.normal(next(ks), (BATCH, SEQ, D_MODEL), jnp.float32)
    mem = jax.random.normal(next(ks), (BATCH, MEM_LEN, D_MODEL), jnp.float32)
    offsets = jax.random.randint(next(ks), (BATCH, 1), 0, MAX_POS_OFFSET, dtype=jnp.int32)
    positions = offsets + jnp.arange(SEQ, dtype=jnp.int32)[None, :]

    mla_w_in = w((N_A, D_MODEL, MLA_Q_RANK + MLA_KV_RANK + MLA_ROPE), D_MODEL)
    mla_q_norm = gain((N_A, MLA_Q_RANK))
    mla_kv_norm = gain((N_A, MLA_KV_RANK))
    mla_w_uq = w((N_A, MLA_Q_RANK, MLA_HEADS * (MLA_NOPE + MLA_ROPE)), MLA_Q_RANK)
    mla_w_ukv = w((N_A, MLA_KV_RANK, MLA_HEADS * (MLA_NOPE + MLA_V)), MLA_KV_RANK)
    mla_w_o = w((N_A, MLA_HEADS * MLA_V, D_MODEL), MLA_HEADS * MLA_V)

    gdn_w_in = w((N_B, D_MODEL, GDN_PROJ), D_MODEL)
    gdn_conv_w = w((N_B, GDN_CONV, GDN_QKV), GDN_CONV)
    gdn_a_log = jnp.log(jax.random.uniform(next(ks), (N_B, GDN_HEADS), jnp.float32, 1.0, 16.0))
    dt = jnp.exp(jax.random.uniform(next(ks), (N_B, GDN_HEADS), jnp.float32,
                                    float(np.log(1e-3)), float(np.log(1e-1))))
    gdn_dt_bias = dt + jnp.log(-jnp.expm1(-dt))
    gdn_o_norm = gain((N_B, GDN_DV))
    gdn_w_o = w((N_B, GDN_HEADS * GDN_DV, D_MODEL), GDN_HEADS * GDN_DV)

    sc_w_in = w((N_C, D_MODEL, 3 * SC_WIDTH), D_MODEL)
    sc_conv_w = w((N_C, SC_CONV, SC_WIDTH), SC_CONV)
    sc_w_o = w((N_C, SC_WIDTH, D_MODEL), SC_WIDTH)

    norm_mix = gain((DEPTH, D_MODEL))
    norm_mem = gain((DEPTH, D_MODEL))
    norm_mlp = gain((DEPTH, D_MODEL))
    xa_w_q = w((DEPTH, D_MODEL, X_HEADS * X_HEAD_DIM), D_MODEL)
    xa_w_kv = w((DEPTH, D_MODEL, 2 * X_HEADS * X_HEAD_DIM), D_MODEL)
    xa_w_o = w((DEPTH, X_HEADS * X_HEAD_DIM, D_MODEL), X_HEADS * X_HEAD_DIM)
    mlp_w1 = w((DEPTH, D_MODEL, D_FF), D_MODEL)
    mlp_w2 = w((DEPTH, D_FF, D_MODEL), D_FF)
    mem_norm = gain((D_MODEL,))
    final_norm = gain((D_MODEL,))

    return {
        "x": x, "mem": mem, "positions": positions,
        "mla_w_in": mla_w_in, "mla_q_norm": mla_q_norm, "mla_kv_norm": mla_kv_norm,
        "mla_w_uq": mla_w_uq, "mla_w_ukv": mla_w_ukv, "mla_w_o": mla_w_o,
        "gdn_w_in": gdn_w_in, "gdn_conv_w": gdn_conv_w, "gdn_a_log": gdn_a_log,
        "gdn_dt_bias": gdn_dt_bias, "gdn_o_norm": gdn_o_norm, "gdn_w_o": gdn_w_o,
        "sc_w_in": sc_w_in, "sc_conv_w": sc_conv_w, "sc_w_o": sc_w_o,
        "norm_mix": norm_mix, "norm_mem": norm_mem, "norm_mlp": norm_mlp,
        "xa_w_q": xa_w_q, "xa_w_kv": xa_w_kv, "xa_w_o": xa_w_o,
        "mlp_w1": mlp_w1, "mlp_w2": mlp_w2,
        "mem_norm": mem_norm, "final_norm": final_norm,
    }


def reference(x, mem, positions,
              mla_w_in, mla_q_norm, mla_kv_norm, mla_w_uq, mla_w_ukv, mla_w_o,
              gdn_w_in, gdn_conv_w, gdn_a_log, gdn_dt_bias, gdn_o_norm, gdn_w_o,
              sc_w_in, sc_conv_w, sc_w_o,
              norm_mix, norm_mem, norm_mlp,
              xa_w_q, xa_w_kv, xa_w_o,
              mlp_w1, mlp_w2,
              mem_norm, final_norm):
    cos, sin = rope_tables(positions)
    mem_n = rms_norm(mem, mem_norm)
    for i in range(DEPTH):
        j = i // N_MIXERS
        kind = i % N_MIXERS
        h = rms_norm(x, norm_mix[i])
        if kind == 0:
            y = mla_mixer(h, cos, sin, mla_w_in[j], mla_q_norm[j], mla_kv_norm[j],
                          mla_w_uq[j], mla_w_ukv[j], mla_w_o[j])
        elif kind == 1:
            y = gdn_mixer(h, gdn_w_in[j], gdn_conv_w[j], gdn_a_log[j], gdn_dt_bias[j],
                          gdn_o_norm[j], gdn_w_o[j])
        else:
            y = short_conv_mixer(h, sc_w_in[j], sc_conv_w[j], sc_w_o[j])
        x = x + y
        x = x + memory_cross_attention(rms_norm(x, norm_mem[i]), mem_n,
                                       xa_w_q[i], xa_w_kv[i], xa_w_o[i])
        x = x + relu2_mlp(rms_norm(x, norm_mlp[i]), mlp_w1[i], mlp_w2[i])
    return rms_norm(x, final_norm)
```

```python
import functools

import jax
import jax.numpy as jnp
from jax import lax
from jax.experimental import pallas as pl
from jax.experimental.pallas import tpu as pltpu

F32 = jnp.float32
BF16 = jnp.bfloat16

EPS = 1e-6
ROPE_THETA = 10000.0
N_MIXERS = 3

MLA_HEADS = 8
MLA_NOPE = 128
MLA_ROPE = 64
MLA_V = 128
MLA_Q_RANK = 384
MLA_KV_RANK = 256

GDN_HEADS = 8
GDN_DK = 128
GDN_DV = 128
GDN_CONV = 4
GDN_CHUNK = 64

SC_CONV = 3
X_HEADS = 4

LANES = 128
SUBLANES = 8
VMEM_LIMIT_BYTES = 56 * 1024 * 1024

TM_TABLE = 1024
TM_MEM = 512
TM_TAIL = 512
TM_MLP = 512
TM_MLA_PROJ = 512
TQ_MLA = 256
TM_SC = 512
TM_GDN_PROJ = 256
T_GDN = 128
FF_CHUNK = 1024

NT_DIMS = (((1,), (1,)), ((), ()))
TN_DIMS = (((0,), (0,)), ((), ()))


def _params(*sem):
    return pltpu.CompilerParams(dimension_semantics=sem, vmem_limit_bytes=VMEM_LIMIT_BYTES)


def _resident(shape):
    nd = len(shape)
    return pl.BlockSpec(shape, lambda *_: (0,) * nd, pipeline_mode=pl.Buffered(1))


def _rms(x, g):
    return x * lax.rsqrt(jnp.mean(x * x, axis=-1, keepdims=True) + EPS) * g


def _dot(a, b):
    return jnp.dot(a, b, preferred_element_type=F32)


def _silu(x):
    return x * jax.nn.sigmoid(x)


def _rope_table_kernel(pos_ref, freq_ref, cos_ref, sin_ref):
    ang = pos_ref[...] * freq_ref[...]
    live = lax.broadcasted_iota(jnp.int32, ang.shape, 1) < MLA_ROPE
    cos_ref[...] = jnp.where(live, jnp.cos(ang), 0.0)
    sin_ref[...] = jnp.where(live, jnp.sin(ang), 0.0)


def _rope_tables(positions):
    n = positions.size
    pos = jnp.broadcast_to(positions.reshape(n, 1).astype(F32), (n, LANES))
    inv_freq = ROPE_THETA ** (-jnp.arange(0, MLA_ROPE, 2, dtype=F32) / MLA_ROPE)
    freq = jnp.concatenate([inv_freq, inv_freq, jnp.zeros((LANES - MLA_ROPE,), F32)])[None, :]
    tm = TM_TABLE
    row = pl.BlockSpec((tm, LANES), lambda i: (i, 0))
    return pl.pallas_call(
        _rope_table_kernel,
        grid=(n // tm,),
        in_specs=[row, pl.BlockSpec((1, LANES), lambda i: (0, 0))],
        out_specs=[row, row],
        out_shape=[jax.ShapeDtypeStruct((n, LANES), F32)] * 2,
        compiler_params=_params("parallel"),
        name="rope_tables",
    )(pos, freq)


def _mem_kv_kernel(mem_ref, g_ref, w_ref, out_ref):
    mn = _rms(mem_ref[...], g_ref[...]).astype(BF16)
    out_ref[0] = _dot(mn, w_ref[0]).astype(BF16)


def _mem_kv(mem2d, mem_norm, w_kv):
    n, d = mem2d.shape
    depth, _, d2 = w_kv.shape
    tm = TM_MEM
    return pl.pallas_call(
        _mem_kv_kernel,
        grid=(depth, n // tm),
        in_specs=[
            pl.BlockSpec((tm, d), lambda l, i: (i, 0)),
            pl.BlockSpec((1, d), lambda l, i: (0, 0)),
            pl.BlockSpec((1, d, d2), lambda l, i: (l, 0, 0)),
        ],
        out_specs=pl.BlockSpec((1, tm, d2), lambda l, i: (l, i, 0)),
        out_shape=jax.ShapeDtypeStruct((depth, n, d2), BF16),
        compiler_params=_params("parallel", "parallel"),
        name="mem_kv",
    )(mem2d, mem_norm[None, :], w_kv)


def _tail_kernel(x_ref, pre_ref, wmix_ref, g_ref, wq_ref, kv_ref, wo_ref, out_ref):
    d = x_ref.shape[-1]
    dh = d // X_HEADS
    x = x_ref[...] + _dot(pre_ref[...], wmix_ref[...])
    h = _rms(x, g_ref[...]).astype(BF16)
    q = (_dot(h, wq_ref[...]) * (dh ** -0.5)).astype(BF16)
    heads = []
    for hd in range(X_HEADS):
        qh = q[:, hd * dh:(hd + 1) * dh]
        kh = kv_ref[:, hd * dh:(hd + 1) * dh]
        vh = kv_ref[:, d + hd * dh:d + (hd + 1) * dh]
        s = lax.dot_general(qh, kh, NT_DIMS, preferred_element_type=F32)
        p = jnp.exp(s - jnp.max(s, axis=-1, keepdims=True))
        l = jnp.sum(p, axis=-1, keepdims=True)
        heads.append((_dot(p.astype(BF16), vh) / l).astype(BF16))
    o = jnp.concatenate(heads, axis=-1)
    out_ref[...] = x + _dot(o, wo_ref[...])


def _tail(x, pre, w_mix, g_mem, w_q, kv_all, layer, w_o):
    b, s, d = x.shape
    m = kv_all.shape[2]
    tm = TM_TAIL
    row = pl.BlockSpec((None, tm, d), lambda bi, i: (bi, i, 0))
    return pl.pallas_call(
        _tail_kernel,
        grid=(b, s // tm),
        in_specs=[
            row, row, _resident((d, d)),
            pl.BlockSpec((1, d), lambda bi, i: (0, 0)),
            _resident((d, d)),
            pl.BlockSpec((None, None, m, 2 * d), lambda bi, i: (layer, bi, 0, 0)),
            _resident((d, d)),
        ],
        out_specs=row,
        out_shape=jax.ShapeDtypeStruct((b, s, d), F32),
        compiler_params=_params("parallel", "parallel"),
        name="tail",
    )(x, pre, w_mix, g_mem[None, :], w_q, kv_all, w_o)


def _mlp_kernel(x_ref, g_ref, w1_ref, w2_ref, gf_ref, out_ref, *, final):
    x = x_ref[...]
    h = _rms(x, g_ref[...]).astype(BF16)
    acc = x
    for c in range(0, w1_ref.shape[1], FF_CHUNK):
        a = jnp.maximum(_dot(h, w1_ref[:, c:c + FF_CHUNK]), 0.0)
        acc = acc + _dot((a * a).astype(BF16), w2_ref[c:c + FF_CHUNK, :])
    if final:
        acc = _rms(acc, gf_ref[...])
    out_ref[...] = acc


def _mlp(x2d, g, w1, w2, g_final, final):
    n, d = x2d.shape
    ff = w1.shape[1]
    tm = TM_MLP
    row = pl.BlockSpec((tm, d), lambda i: (i, 0))
    vec = pl.BlockSpec((1, d), lambda i: (0, 0))
    return pl.pallas_call(
        functools.partial(_mlp_kernel, final=final),
        grid=(n // tm,),
        in_specs=[row, vec, _resident((d, ff)), _resident((ff, d)), vec],
        out_specs=row,
        out_shape=jax.ShapeDtypeStruct((n, d), F32),
        compiler_params=_params("parallel"),
        name="mlp",
    )(x2d, g[None, :], w1, w2, g_final[None, :])


def _mla_weights(w_in, w_uq, w_ukv):
    half = MLA_ROPE // 2

    def swap_cols(w):
        return jnp.concatenate([-w[..., half:], w[..., :half]], axis=-1)

    def pad_cols(w):
        return jnp.concatenate([w, jnp.zeros(w.shape[:-1] + (LANES - MLA_ROPE,), w.dtype)], axis=-1)

    d = w_in.shape[0]
    lat = MLA_Q_RANK + MLA_KV_RANK
    kr = w_in[:, lat:]
    w_in_x = jnp.concatenate([w_in[:, :lat], pad_cols(kr), pad_cols(swap_cols(kr))], axis=1)
    uq = w_uq.reshape(MLA_Q_RANK, MLA_HEADS, MLA_NOPE + MLA_ROPE)
    qn = uq[:, :, :MLA_NOPE].reshape(MLA_Q_RANK, -1)
    qr = uq[:, :, MLA_NOPE:]
    w_uq_x = jnp.concatenate(
        [qn, pad_cols(qr).reshape(MLA_Q_RANK, -1), pad_cols(swap_cols(qr)).reshape(MLA_Q_RANK, -1)], axis=1)
    ukv = w_ukv.reshape(MLA_KV_RANK, MLA_HEADS, MLA_NOPE + MLA_V)
    w_ukv_x = jnp.concatenate(
        [ukv[:, :, :MLA_NOPE].reshape(MLA_KV_RANK, -1), ukv[:, :, MLA_NOPE:].reshape(MLA_KV_RANK, -1)], axis=1)
    del d
    return w_in_x.astype(BF16), w_uq_x.astype(BF16), w_ukv_x.astype(BF16)


def _mla_proj_kernel(x_ref, g_ref, win_ref, gq_ref, gkv_ref, wuq_ref, wukv_ref, cos_ref, sin_ref,
                     qn_ref, qr_ref, kn_ref, kr_ref, v_ref):
    hn = MLA_HEADS * MLA_NOPE
    lat = MLA_Q_RANK + MLA_KV_RANK
    scale = (MLA_NOPE + MLA_ROPE) ** -0.5
    h = _rms(x_ref[...], g_ref[...]).astype(BF16)
    z = _dot(h, win_ref[...])
    cos = cos_ref[...]
    sin = sin_ref[...]
    kr_ref[...] = (z[:, lat:lat + LANES] * cos + z[:, lat + LANES:lat + 2 * LANES] * sin).astype(BF16)
    cq = _rms(z[:, :MLA_Q_RANK], gq_ref[...]).astype(BF16)
    ckv = _rms(z[:, MLA_Q_RANK:lat], gkv_ref[...]).astype(BF16)
    q = _dot(cq, wuq_ref[...]) * scale
    qn_ref[...] = q[:, :hn].astype(BF16)
    for hd in range(MLA_HEADS):
        a = q[:, hn + hd * LANES:hn + (hd + 1) * LANES]
        bsw = q[:, 2 * hn + hd * LANES:2 * hn + (hd + 1) * LANES]
        qr_ref[:, hd * LANES:(hd + 1) * LANES] = (a * cos + bsw * sin).astype(BF16)
    kv = _dot(ckv, wukv_ref[...])
    kn_ref[...] = kv[:, :hn].astype(BF16)
    v_ref[...] = kv[:, hn:].astype(BF16)


def _mla_attn_kernel(qn_ref, qr_ref, kn_ref, kr_ref, v_ref, o_ref):
    tq = qn_ref.shape[0]
    tk = tq
    i = pl.program_id(2)
    q = jnp.concatenate([qn_ref[...], qr_ref[...]], axis=-1)

    def block(j, carry, masked):
        m, l, acc = carry
        ks = pl.multiple_of(j * tk, tk)
        k = jnp.concatenate([kn_ref[pl.ds(ks, tk), :], kr_ref[pl.ds(ks, tk), :]], axis=-1)
        s = lax.dot_general(q, k, NT_DIMS, preferred_element_type=F32)
        if masked:
            rows = lax.broadcasted_iota(jnp.int32, s.shape, 0)
            cols = lax.broadcasted_iota(jnp.int32, s.shape, 1)
            s = jnp.where(rows >= cols, s, -jnp.inf)
        m_new = jnp.maximum(m, jnp.max(s, axis=-1, keepdims=True))
        alpha = jnp.exp(m - m_new)
        p = jnp.exp(s - m_new)
        l = alpha * l + jnp.sum(p, axis=-1, keepdims=True)
        acc = alpha * acc + _dot(p.astype(BF16), v_ref[pl.ds(ks, tk), :])
        return m_new, l, acc

    init = (jnp.full((tq, 1), -jnp.inf, F32), jnp.zeros((tq, 1), F32), jnp.zeros((tq, MLA_V), F32))
    carry = lax.fori_loop(0, i, lambda j, c: block(j, c, False), init)
    _, l, acc = block(i, carry, True)
    o_ref[...] = (acc / l).astype(BF16)


def _mla_mixer(x, g, w_in_x, gq, gkv, w_uq_x, w_ukv_x, cos_t, sin_t):
    b, s, d = x.shape
    n = b * s
    hn = MLA_HEADS * MLA_NOPE
    tm = TM_MLA_PROJ
    row = lambda w: pl.BlockSpec((tm, w), lambda i: (i, 0))
    vec = lambda w: pl.BlockSpec((1, w), lambda i: (0, 0))
    qn, qr, kn, kr, v = pl.pallas_call(
        _mla_proj_kernel,
        grid=(n // tm,),
        in_specs=[row(d), vec(d), _resident(w_in_x.shape), vec(MLA_Q_RANK), vec(MLA_KV_RANK),
                  _resident(w_uq_x.shape), _resident(w_ukv_x.shape), row(LANES), row(LANES)],
        out_specs=[row(hn), row(hn), row(hn), row(LANES), row(hn)],
        out_shape=[jax.ShapeDtypeStruct((n, hn), BF16), jax.ShapeDtypeStruct((n, hn), BF16),
                   jax.ShapeDtypeStruct((n, hn), BF16), jax.ShapeDtypeStruct((n, LANES), BF16),
                   jax.ShapeDtypeStruct((n, hn), BF16)],
        compiler_params=_params("parallel"),
        name="mla_proj",
    )(x.reshape(n, d), g[None, :], w_in_x, gq[None, :], gkv[None, :], w_uq_x, w_ukv_x, cos_t, sin_t)

    tq = TQ_MLA
    qblk = pl.BlockSpec((None, tq, LANES), lambda bi, hd, i: (bi, i, hd))
    kblk = pl.BlockSpec((None, s, LANES), lambda bi, hd, i: (bi, 0, hd))
    return pl.pallas_call(
        _mla_attn_kernel,
        grid=(b, MLA_HEADS, s // tq),
        in_specs=[qblk, qblk, kblk, pl.BlockSpec((None, s, LANES), lambda bi, hd, i: (bi, 0, 0)), kblk],
        out_specs=qblk,
        out_shape=jax.ShapeDtypeStruct((b, s, hn), BF16),
        compiler_params=_params("parallel", "parallel", "arbitrary"),
        name="mla_attn",
    )(qn.reshape(b, s, hn), qr.reshape(b, s, hn), kn.reshape(b, s, hn), kr.reshape(b, s, LANES),
      v.reshape(b, s, hn))


def _causal_conv(z, w_ref, col, buf_ref, first):
    tm, c = z.shape
    k = w_ref.shape[0]

    @pl.when(first)
    def _():
        buf_ref[0:SUBLANES, :] = jnp.zeros((SUBLANES, c), F32)

    buf_ref[SUBLANES:SUBLANES + tm, :] = z
    y = w_ref[k - 1:k, col:col + c] * z
    for j in range(1, k):
        y = y + w_ref[k - 1 - j:k - j, col:col + c] * buf_ref[SUBLANES - j:SUBLANES - j + tm, :]
    buf_ref[0:SUBLANES, :] = z[tm - SUBLANES:, :]
    return y


def _sc_kernel(x_ref, g_ref, win_ref, cw_ref, out_ref, buf_ref):
    w = out_ref.shape[-1]
    h = _rms(x_ref[...], g_ref[...]).astype(BF16)
    cu = _dot(h, win_ref[:, w:2 * w]) * _dot(h, win_ref[:, 2 * w:3 * w])
    y = _causal_conv(cu, cw_ref, 0, buf_ref, pl.program_id(1) == 0)
    out_ref[...] = (_dot(h, win_ref[:, 0:w]) * y).astype(BF16)


def _sc_mixer(x, g, w_in, conv_w):
    b, s, d = x.shape
    w = conv_w.shape[1]
    tm = TM_SC
    row = lambda c: pl.BlockSpec((None, tm, c), lambda bi, i: (bi, i, 0))
    return pl.pallas_call(
        _sc_kernel,
        grid=(b, s // tm),
        in_specs=[row(d), pl.BlockSpec((1, d), lambda bi, i: (0, 0)), _resident(w_in.shape),
                  pl.BlockSpec(conv_w.shape, lambda bi, i: (0, 0))],
        out_specs=row(w),
        out_shape=jax.ShapeDtypeStruct((b, s, w), BF16),
        scratch_shapes=[pltpu.VMEM((tm + SUBLANES, w), F32)],
        compiler_params=_params("parallel", "arbitrary"),
        name="short_conv",
    )(x, g[None, :], w_in, conv_w)


def _gdn_proj_kernel(x_ref, g_ref, win_ref, cw_ref, alog_ref, dtb_ref,
                     q_ref, k_ref, v_ref, gate_ref, bg_ref, buf_ref):
    hk = GDN_HEADS * GDN_DK
    first = pl.program_id(1) == 0
    h = _rms(x_ref[...], g_ref[...]).astype(BF16)
    for part, ref in enumerate((q_ref, k_ref, v_ref)):
        z = _dot(h, win_ref[:, part * hk:(part + 1) * hk])
        y = _silu(_causal_conv(z, cw_ref, part * hk, buf_ref.at[part], first))
        if part < 2:
            post = GDN_DK ** -0.5 if part == 0 else 1.0
            for hd in range(GDN_HEADS):
                yh = y[:, hd * GDN_DK:(hd + 1) * GDN_DK]
                inv = lax.rsqrt(jnp.sum(yh * yh, axis=-1, keepdims=True) + EPS)
                ref[:, hd * GDN_DK:(hd + 1) * GDN_DK] = yh * inv * post
        else:
            ref[...] = y
    gate_ref[...] = _dot(h, win_ref[:, 3 * hk:4 * hk])
    zb = _dot(h, win_ref[:, 4 * hk:4 * hk + LANES])
    beta = jax.nn.sigmoid(zb)
    t = zb + dtb_ref[...]
    softplus = jnp.maximum(t, 0.0) + jnp.log1p(jnp.exp(-jnp.abs(t)))
    decay = -jnp.exp(alog_ref[...]) * softplus
    lane = lax.broadcasted_iota(jnp.int32, zb.shape, 1)
    bg_ref[...] = jnp.where(lane < GDN_HEADS, beta, decay)


def _unit_lower_inverse(m):
    c = m.shape[0]
    hi = functools.partial(jnp.dot, precision=lax.Precision.HIGHEST, preferred_element_type=F32)
    eye = (lax.broadcasted_iota(jnp.int32, (c, c), 0) == lax.broadcasted_iota(jnp.int32, (c, c), 1)).astype(F32)
    inv = eye - m
    power = m
    width = 2
    while width < c:
        power = hi(power, power)
        inv = inv + hi(inv, power)
        width *= 2
    return inv


def _gdn_chunk_kernel(q_ref, k_ref, v_ref, gate_ref, bg_ref, onorm_ref, out_ref, state_ref):
    c = GDN_CHUNK
    t = q_ref.shape[0]

    @pl.when(pl.program_id(1) == 0)
    def _():
        state_ref[...] = jnp.zeros(state_ref.shape, F32)

    rows = lax.broadcasted_iota(jnp.int32, (c, c), 0)
    cols = lax.broadcasted_iota(jnp.int32, (c, c), 1)
    tri = rows >= cols
    strict = rows > cols
    hi = functools.partial(jnp.dot, precision=lax.Precision.HIGHEST, preferred_element_type=F32)
    onorm = onorm_ref[...]
    for r0 in range(0, t, c):
        bg = bg_ref[r0:r0 + c, :]
        gcum = hi(tri.astype(F32), bg)
        gcum_t = gcum.T
        for hd in range(GDN_HEADS):
            ks = slice(hd * GDN_DK, (hd + 1) * GDN_DK)
            vs = slice(hd * GDN_DV, (hd + 1) * GDN_DV)
            q = q_ref[r0:r0 + c, ks]
            k = k_ref[r0:r0 + c, ks]
            v = v_ref[r0:r0 + c, vs]
            beta = bg[:, hd:hd + 1]
            gcol = gcum[:, GDN_HEADS + hd:GDN_HEADS + hd + 1]
            grow = gcum_t[GDN_HEADS + hd:GDN_HEADS + hd + 1, :]
            decay = jnp.where(tri, jnp.exp(jnp.where(tri, gcol - grow, 0.0)), 0.0)
            kb = k * beta
            kbf = k.astype(BF16)
            kk = lax.dot_general(kb.astype(BF16), kbf, NT_DIMS, preferred_element_type=F32)
            t_inv = _unit_lower_inverse(jnp.where(strict, kk * decay, 0.0)).astype(BF16)
            eg = jnp.exp(gcol)
            u = _dot(t_inv, (v * beta).astype(BF16))
            w = _dot(t_inv, (kb * eg).astype(BF16))
            attn = lax.dot_general(q.astype(BF16), kbf, NT_DIMS, preferred_element_type=F32) * decay
            state = state_ref[hd]
            sb = state.astype(BF16)
            v_new = u - _dot(w.astype(BF16), sb)
            vnb = v_new.astype(BF16)
            o = _dot((q * eg).astype(BF16), sb) + _dot(attn.astype(BF16), vnb)
            g_last = gcol[c - 1:c, :]
            kd = (k * jnp.exp(g_last - gcol)).astype(BF16)
            state_ref[hd] = state * jnp.exp(g_last) + lax.dot_general(
                kd, vnb, TN_DIMS, preferred_element_type=F32)
            on = _rms(o, onorm)
            out_ref[r0:r0 + c, vs] = (on * _silu(gate_ref[r0:r0 + c, vs])).astype(BF16)


def _gdn_mixer(x, g, w_in_x, conv_w, a_log, dt_bias, o_norm):
    b, s, d = x.shape
    hk = GDN_HEADS * GDN_DK
    lanes_pad = jnp.zeros((LANES - 2 * GDN_HEADS,), F32)
    zeros_h = jnp.zeros((GDN_HEADS,), F32)
    alog_row = jnp.concatenate([zeros_h, a_log, lanes_pad])[None, :]
    dtb_row = jnp.concatenate([zeros_h, dt_bias, lanes_pad])[None, :]
    tm = TM_GDN_PROJ
    row = lambda tt, c: pl.BlockSpec((None, tt, c), lambda bi, i: (bi, i, 0))
    vec = lambda c: pl.BlockSpec((1, c), lambda bi, i: (0, 0))
    q, k, v, gate, bg = pl.pallas_call(
        _gdn_proj_kernel,
        grid=(b, s // tm),
        in_specs=[row(tm, d), vec(d), _resident(w_in_x.shape),
                  pl.BlockSpec(conv_w.shape, lambda bi, i: (0, 0)), vec(LANES), vec(LANES)],
        out_specs=[row(tm, hk), row(tm, hk), row(tm, hk), row(tm, hk), row(tm, LANES)],
        out_shape=[jax.ShapeDtypeStruct((b, s, hk), F32)] * 4 + [jax.ShapeDtypeStruct((b, s, LANES), F32)],
        scratch_shapes=[pltpu.VMEM((3, tm + SUBLANES, hk), F32)],
        compiler_params=_params("parallel", "arbitrary"),
        name="gdn_proj",
    )(x, g[None, :], w_in_x, conv_w, alog_row, dtb_row)

    tt = T_GDN
    return pl.pallas_call(
        _gdn_chunk_kernel,
        grid=(b, s // tt),
        in_specs=[row(tt, hk), row(tt, hk), row(tt, hk), row(tt, hk), row(tt, LANES), vec(GDN_DV)],
        out_specs=row(tt, hk),
        out_shape=jax.ShapeDtypeStruct((b, s, hk), BF16),
        scratch_shapes=[pltpu.VMEM((GDN_HEADS, GDN_DK, GDN_DV), F32)],
        compiler_params=_params("parallel", "arbitrary"),
        name="gdn_chunk",
    )(q, k, v, gate, bg, o_norm[None, :])


def kernel(x, mem, positions, mla_w_in, mla_q_norm, mla_kv_norm, mla_w_uq, mla_w_ukv, mla_w_o, gdn_w_in, gdn_conv_w, gdn_a_log, gdn_dt_bias, gdn_o_norm, gdn_w_o, sc_w_in, sc_conv_w, sc_w_o, norm_mix, norm_mem, norm_mlp, xa_w_q, xa_w_kv, xa_w_o, mlp_w1, mlp_w2, mem_norm, final_norm):
    b, s, d = x.shape
    depth = norm_mix.shape[0]
    m = mem.shape[1]
    cos_t, sin_t = _rope_tables(positions)
    kv_all = _mem_kv(mem.reshape(b * m, d), mem_norm, xa_w_kv.astype(BF16)).reshape(depth, b, m, 2 * d)
    for i in range(depth):
        j = i // N_MIXERS
        kind = i % N_MIXERS
        if kind == 0:
            w_in_x, w_uq_x, w_ukv_x = _mla_weights(mla_w_in[j], mla_w_uq[j], mla_w_ukv[j])
            pre = _mla_mixer(x, norm_mix[i], w_in_x, mla_q_norm[j], mla_kv_norm[j], w_uq_x, w_ukv_x,
                             cos_t, sin_t)
            w_mix = mla_w_o[j]
        elif kind == 1:
            pad = jnp.zeros((d, LANES - 2 * GDN_HEADS), F32)
            w_in_x = jnp.concatenate([gdn_w_in[j], pad], axis=1).astype(BF16)
            pre = _gdn_mixer(x, norm_mix[i], w_in_x, gdn_conv_w[j], gdn_a_log[j], gdn_dt_bias[j],
                             gdn_o_norm[j])
            w_mix = gdn_w_o[j]
        else:
            pre = _sc_mixer(x, norm_mix[i], sc_w_in[j].astype(BF16), sc_conv_w[j])
            w_mix = sc_w_o[j]
        x = _tail(x, pre, w_mix.astype(BF16), norm_mem[i], xa_w_q[i].astype(BF16), kv_all, i,
                  xa_w_o[i].astype(BF16))
        x = _mlp(x.reshape(b * s, d), norm_mlp[i], mlp_w1[i].astype(BF16), mlp_w2[i].astype(BF16),
                 final_norm, i == depth - 1).reshape(b, s, d)
    return x
```

```python
import functools

import jax
import jax.numpy as jnp
from jax import lax
from jax.experimental import pallas as pl
from jax.experimental.pallas import tpu as pltpu

F32 = jnp.float32
BF16 = jnp.bfloat16

EPS = 1e-6
ROPE_THETA = 10000.0
N_MIXERS = 3

MLA_HEADS = 8
MLA_NOPE = 128
MLA_ROPE = 64
MLA_V = 128
MLA_Q_RANK = 384
MLA_KV_RANK = 256

GDN_HEADS = 8
GDN_DK = 128
GDN_DV = 128
GDN_CONV = 4
GDN_CHUNK = 64

SC_CONV = 3
X_HEADS = 4

LANES = 128
SUBLANES = 8
VMEM_LIMIT_BYTES = 56 * 1024 * 1024

TM_TABLE = 1024
TM_MEM = 512
TM_TAIL = 512
TM_MLP = 512
TM_MLA_PROJ = 512
TQ_MLA = 256
TM_SC = 512
TM_GDN_PROJ = 256
T_GDN = 128
FF_CHUNK = 1024

NT_DIMS = (((1,), (1,)), ((), ()))
TN_DIMS = (((0,), (0,)), ((), ()))


def _params(*sem):
    return pltpu.CompilerParams(dimension_semantics=sem, vmem_limit_bytes=VMEM_LIMIT_BYTES)


def _resident(shape):
    nd = len(shape)
    return pl.BlockSpec(shape, lambda *_: (0,) * nd, pipeline_mode=pl.Buffered(1))


def _rms(x, g):
    return x * lax.rsqrt(jnp.mean(x * x, axis=-1, keepdims=True) + EPS) * g


def _dot(a, b):
    return jnp.dot(a, b, preferred_element_type=F32)


def _silu(x):
    return x * jax.nn.sigmoid(x)


def _rope_table_kernel(pos_ref, freq_ref, cos_ref, sin_ref):
    ang = pos_ref[...] * freq_ref[...]
    live = lax.broadcasted_iota(jnp.int32, ang.shape, 1) < MLA_ROPE
    cos_ref[...] = jnp.where(live, jnp.cos(ang), 0.0)
    sin_ref[...] = jnp.where(live, jnp.sin(ang), 0.0)


def _rope_tables(positions):
    n = positions.size
    pos = jnp.broadcast_to(positions.reshape(n, 1).astype(F32), (n, LANES))
    inv_freq = ROPE_THETA ** (-jnp.arange(0, MLA_ROPE, 2, dtype=F32) / MLA_ROPE)
    freq = jnp.concatenate([inv_freq, inv_freq, jnp.zeros((LANES - MLA_ROPE,), F32)])[None, :]
    tm = TM_TABLE
    row = pl.BlockSpec((tm, LANES), lambda i: (i, 0))
    return pl.pallas_call(
        _rope_table_kernel,
        grid=(n // tm,),
        in_specs=[row, pl.BlockSpec((1, LANES), lambda i: (0, 0))],
        out_specs=[row, row],
        out_shape=[jax.ShapeDtypeStruct((n, LANES), F32)] * 2,
        compiler_params=_params("parallel"),
        name="rope_tables",
    )(pos, freq)


def _mem_kv_kernel(mem_ref, g_ref, w_ref, out_ref):
    mn = _rms(mem_ref[...], g_ref[...]).astype(BF16)
    out_ref[0] = _dot(mn, w_ref[0]).astype(BF16)


def _mem_kv(mem2d, mem_norm, w_kv):
    n, d = mem2d.shape
    depth, _, d2 = w_kv.shape
    tm = TM_MEM
    return pl.pallas_call(
        _mem_kv_kernel,
        grid=(depth, n // tm),
        in_specs=[
            pl.BlockSpec((tm, d), lambda l, i: (i, 0)),
            pl.BlockSpec((1, d), lambda l, i: (0, 0)),
            pl.BlockSpec((1, d, d2), lambda l, i: (l, 0, 0)),
        ],
        out_specs=pl.BlockSpec((1, tm, d2), lambda l, i: (l, i, 0)),
        out_shape=jax.ShapeDtypeStruct((depth, n, d2), BF16),
        compiler_params=_params("parallel", "parallel"),
        name="mem_kv",
    )(mem2d, mem_norm[None, :], w_kv)


def _tail_kernel(x_ref, pre_ref, wmix_ref, g_ref, wq_ref, kv_ref, wo_ref, out_ref):
    d = x_ref.shape[-1]
    dh = d // X_HEADS
    x = x_ref[...] + _dot(pre_ref[...], wmix_ref[...])
    h = _rms(x, g_ref[...]).astype(BF16)
    q = (_dot(h, wq_ref[...]) * (dh ** -0.5)).astype(BF16)
    heads = []
    for hd in range(X_HEADS):
        qh = q[:, hd * dh:(hd + 1) * dh]
        kh = kv_ref[:, hd * dh:(hd + 1) * dh]
        vh = kv_ref[:, d + hd * dh:d + (hd + 1) * dh]
        s = lax.dot_general(qh, kh, NT_DIMS, preferred_element_type=F32)
        p = jnp.exp(s - jnp.max(s, axis=-1, keepdims=True))
        l = jnp.sum(p, axis=-1, keepdims=True)
        heads.append((_dot(p.astype(BF16), vh) / l).astype(BF16))
    o = jnp.concatenate(heads, axis=-1)
    out_ref[...] = x + _dot(o, wo_ref[...])


def _tail(x, pre, w_mix, g_mem, w_q, kv_all, layer, w_o):
    b, s, d = x.shape
    m = kv_all.shape[2]
    tm = TM_TAIL
    row = pl.BlockSpec((None, tm, d), lambda bi, i: (bi, i, 0))
    return pl.pallas_call(
        _tail_kernel,
        grid=(b, s // tm),
        in_specs=[
            row, row, _resident((d, d)),
            pl.BlockSpec((1, d), lambda bi, i: (0, 0)),
            _resident((d, d)),
            pl.BlockSpec((None, None, m, 2 * d), lambda bi, i: (layer, bi, 0, 0)),
            _resident((d, d)),
        ],
        out_specs=row,
        out_shape=jax.ShapeDtypeStruct((b, s, d), F32),
        compiler_params=_params("parallel", "parallel"),
        name="tail",
    )(x, pre, w_mix, g_mem[None, :], w_q, kv_all, w_o)


def _mlp_kernel(x_ref, g_ref, w1_ref, w2_ref, gf_ref, out_ref, *, final):
    x = x_ref[...]
    h = _rms(x, g_ref[...]).astype(BF16)
    acc = x
    for c in range(0, w1_ref.shape[1], FF_CHUNK):
        a = jnp.maximum(_dot(h, w1_ref[:, c:c + FF_CHUNK]), 0.0)
        acc = acc + _dot((a * a).astype(BF16), w2_ref[c:c + FF_CHUNK, :])
    if final:
        acc = _rms(acc, gf_ref[...])
    out_ref[...] = acc


def _mlp(x2d, g, w1, w2, g_final, final):
    n, d = x2d.shape
    ff = w1.shape[1]
    tm = TM_MLP
    row = pl.BlockSpec((tm, d), lambda i: (i, 0))
    vec = pl.BlockSpec((1, d), lambda i: (0, 0))
    return pl.pallas_call(
        functools.partial(_mlp_kernel, final=final),
        grid=(n // tm,),
        in_specs=[row, vec, _resident((d, ff)), _resident((ff, d)), vec],
        out_specs=row,
        out_shape=jax.ShapeDtypeStruct((n, d), F32),
        compiler_params=_params("parallel"),
        name="mlp",
    )(x2d, g[None, :], w1, w2, g_final[None, :])


def _mla_weights(w_in, w_uq, w_ukv):
    half = MLA_ROPE // 2

    def swap_cols(w):
        return jnp.concatenate([-w[..., half:], w[..., :half]], axis=-1)

    def pad_cols(w):
        return jnp.concatenate([w, jnp.zeros(w.shape[:-1] + (LANES - MLA_ROPE,), w.dtype)], axis=-1)

    d = w_in.shape[0]
    lat = MLA_Q_RANK + MLA_KV_RANK
    kr = w_in[:, lat:]
    w_in_x = jnp.concatenate([w_in[:, :lat], pad_cols(kr), pad_cols(swap_cols(kr))], axis=1)
    uq = w_uq.reshape(MLA_Q_RANK, MLA_HEADS, MLA_NOPE + MLA_ROPE)
    qn = uq[:, :, :MLA_NOPE].reshape(MLA_Q_RANK, -1)
    qr = uq[:, :, MLA_NOPE:]
    w_uq_x = jnp.concatenate(
        [qn, pad_cols(qr).reshape(MLA_Q_RANK, -1), pad_cols(swap_cols(qr)).reshape(MLA_Q_RANK, -1)], axis=1)
    ukv = w_ukv.reshape(MLA_KV_RANK, MLA_HEADS, MLA_NOPE + MLA_V)
    w_ukv_x = jnp.concatenate(
        [ukv[:, :, :MLA_NOPE].reshape(MLA_KV_RANK, -1), ukv[:, :, MLA_NOPE:].reshape(MLA_KV_RANK, -1)], axis=1)
    del d
    return w_in_x.astype(BF16), w_uq_x.astype(BF16), w_ukv_x.astype(BF16)


def _mla_proj_kernel(x_ref, g_ref, win_ref, gq_ref, gkv_ref, wuq_ref, wukv_ref, cos_ref, sin_ref,
                     qn_ref, qr_ref, kn_ref, kr_ref, v_ref):
    hn = MLA_HEADS * MLA_NOPE
    lat = MLA_Q_RANK + MLA_KV_RANK
    scale = (MLA_NOPE + MLA_ROPE) ** -0.5
    h = _rms(x_ref[...], g_ref[...]).astype(BF16)
    z = _dot(h, win_ref[...])
    cos = cos_ref[...]
    sin = sin_ref[...]
    kr_ref[...] = (z[:, lat:lat + LANES] * cos + z[:, lat + LANES:lat + 2 * LANES] * sin).astype(BF16)
    cq = _rms(z[:, :MLA_Q_RANK], gq_ref[...]).astype(BF16)
    ckv = _rms(z[:, MLA_Q_RANK:lat], gkv_ref[...]).astype(BF16)
    q = _dot(cq, wuq_ref[...]) * scale
    qn_ref[...] = q[:, :hn].astype(BF16)
    for hd in range(MLA_HEADS):
        a = q[:, hn + hd * LANES:hn + (hd + 1) * LANES]
        bsw = q[:, 2 * hn + hd * LANES:2 * hn + (hd + 1) * LANES]
        qr_ref[:, hd * LANES:(hd + 1) * LANES] = (a * cos + bsw * sin).astype(BF16)
    kv = _dot(ckv, wukv_ref[...])
    kn_ref[...] = kv[:, :hn].astype(BF16)
    v_ref[...] = kv[:, hn:].astype(BF16)


def _mla_attn_kernel(qn_ref, qr_ref, kn_ref, kr_ref, v_ref, o_ref):
    s_len = qn_ref.shape[0]
    tq = TQ_MLA
    rows = lax.broadcasted_iota(jnp.int32, (tq, tq), 0)
    cols = lax.broadcasted_iota(jnp.int32, (tq, tq), 1)
    causal = rows >= cols
    for q0 in range(0, s_len, tq):
        q1 = q0 + tq
        q = jnp.concatenate([qn_ref[q0:q1, :], qr_ref[q0:q1, :]], axis=-1)
        k_diag = jnp.concatenate([kn_ref[q0:q1, :], kr_ref[q0:q1, :]], axis=-1)
        s_diag = jnp.where(causal, lax.dot_general(q, k_diag, NT_DIMS, preferred_element_type=F32), -jnp.inf)
        m = jnp.max(s_diag, axis=-1, keepdims=True)
        if q0:
            k_past = jnp.concatenate([kn_ref[0:q0, :], kr_ref[0:q0, :]], axis=-1)
            s_past = lax.dot_general(q, k_past, NT_DIMS, preferred_element_type=F32)
            m = jnp.maximum(m, jnp.max(s_past, axis=-1, keepdims=True))
        p_diag = jnp.exp(s_diag - m)
        l = jnp.sum(p_diag, axis=-1, keepdims=True)
        acc = _dot(p_diag.astype(BF16), v_ref[q0:q1, :])
        if q0:
            p_past = jnp.exp(s_past - m)
            l = l + jnp.sum(p_past, axis=-1, keepdims=True)
            acc = acc + _dot(p_past.astype(BF16), v_ref[0:q0, :])
        o_ref[q0:q1, :] = (acc / l).astype(BF16)


def _mla_mixer(x, g, w_in_x, gq, gkv, w_uq_x, w_ukv_x, cos_t, sin_t):
    b, s, d = x.shape
    n = b * s
    hn = MLA_HEADS * MLA_NOPE
    tm = TM_MLA_PROJ
    row = lambda w: pl.BlockSpec((tm, w), lambda i: (i, 0))
    vec = lambda w: pl.BlockSpec((1, w), lambda i: (0, 0))
    qn, qr, kn, kr, v = pl.pallas_call(
        _mla_proj_kernel,
        grid=(n // tm,),
        in_specs=[row(d), vec(d), _resident(w_in_x.shape), vec(MLA_Q_RANK), vec(MLA_KV_RANK),
                  _resident(w_uq_x.shape), _resident(w_ukv_x.shape), row(LANES), row(LANES)],
        out_specs=[row(hn), row(hn), row(hn), row(LANES), row(hn)],
        out_shape=[jax.ShapeDtypeStruct((n, hn), BF16), jax.ShapeDtypeStruct((n, hn), BF16),
                   jax.ShapeDtypeStruct((n, hn), BF16), jax.ShapeDtypeStruct((n, LANES), BF16),
                   jax.ShapeDtypeStruct((n, hn), BF16)],
        compiler_params=_params("parallel"),
        name="mla_proj",
    )(x.reshape(n, d), g[None, :], w_in_x, gq[None, :], gkv[None, :], w_uq_x, w_ukv_x, cos_t, sin_t)

    head = pl.BlockSpec((None, s, LANES), lambda bi, hd: (bi, 0, hd))
    return pl.pallas_call(
        _mla_attn_kernel,
        grid=(b, MLA_HEADS),
        in_specs=[head, head, head, pl.BlockSpec((None, s, LANES), lambda bi, hd: (bi, 0, 0)), head],
        out_specs=head,
        out_shape=jax.ShapeDtypeStruct((b, s, hn), BF16),
        compiler_params=_params("parallel", "parallel"),
        name="mla_attn",
    )(qn.reshape(b, s, hn), qr.reshape(b, s, hn), kn.reshape(b, s, hn), kr.reshape(b, s, LANES),
      v.reshape(b, s, hn))


def _causal_conv(z, w_ref, col, buf_ref, first):
    tm, c = z.shape
    k = w_ref.shape[0]

    @pl.when(first)
    def _():
        buf_ref[0:SUBLANES, :] = jnp.zeros((SUBLANES, c), F32)

    buf_ref[SUBLANES:SUBLANES + tm, :] = z
    y = w_ref[k - 1:k, col:col + c] * z
    for j in range(1, k):
        y = y + w_ref[k - 1 - j:k - j, col:col + c] * buf_ref[SUBLANES - j:SUBLANES - j + tm, :]
    buf_ref[0:SUBLANES, :] = z[tm - SUBLANES:, :]
    return y


def _sc_kernel(x_ref, g_ref, win_ref, cw_ref, out_ref, buf_ref):
    w = out_ref.shape[-1]
    h = _rms(x_ref[...], g_ref[...]).astype(BF16)
    cu = _dot(h, win_ref[:, w:2 * w]) * _dot(h, win_ref[:, 2 * w:3 * w])
    y = _causal_conv(cu, cw_ref, 0, buf_ref, pl.program_id(1) == 0)
    out_ref[...] = (_dot(h, win_ref[:, 0:w]) * y).astype(BF16)


def _sc_mixer(x, g, w_in, conv_w):
    b, s, d = x.shape
    w = conv_w.shape[1]
    tm = TM_SC
    row = lambda c: pl.BlockSpec((None, tm, c), lambda bi, i: (bi, i, 0))
    return pl.pallas_call(
        _sc_kernel,
        grid=(b, s // tm),
        in_specs=[row(d), pl.BlockSpec((1, d), lambda bi, i: (0, 0)), _resident(w_in.shape),
                  pl.BlockSpec(conv_w.shape, lambda bi, i: (0, 0))],
        out_specs=row(w),
        out_shape=jax.ShapeDtypeStruct((b, s, w), BF16),
        scratch_shapes=[pltpu.VMEM((tm + SUBLANES, w), F32)],
        compiler_params=_params("parallel", "arbitrary"),
        name="short_conv",
    )(x, g[None, :], w_in, conv_w)


def _gdn_proj_kernel(x_ref, g_ref, win_ref, cw_ref, alog_ref, dtb_ref,
                     q_ref, k_ref, v_ref, gate_ref, bg_ref, buf_ref):
    hk = GDN_HEADS * GDN_DK
    first = pl.program_id(1) == 0
    h = _rms(x_ref[...], g_ref[...]).astype(BF16)
    for part, ref in enumerate((q_ref, k_ref, v_ref)):
        z = _dot(h, win_ref[:, part * hk:(part + 1) * hk])
        y = _silu(_causal_conv(z, cw_ref, part * hk, buf_ref.at[part], first))
        if part < 2:
            post = GDN_DK ** -0.5 if part == 0 else 1.0
            for hd in range(GDN_HEADS):
                yh = y[:, hd * GDN_DK:(hd + 1) * GDN_DK]
                inv = lax.rsqrt(jnp.sum(yh * yh, axis=-1, keepdims=True) + EPS)
                ref[:, hd * GDN_DK:(hd + 1) * GDN_DK] = (yh * inv * post).astype(BF16)
        else:
            ref[...] = y.astype(BF16)
    gate_ref[...] = _dot(h, win_ref[:, 3 * hk:4 * hk]).astype(BF16)
    zb = _dot(h, win_ref[:, 4 * hk:4 * hk + LANES])
    beta = jax.nn.sigmoid(zb)
    t = zb + dtb_ref[...]
    softplus = jnp.maximum(t, 0.0) + jnp.log1p(jnp.exp(-jnp.abs(t)))
    decay = -jnp.exp(alog_ref[...]) * softplus
    lane = lax.broadcasted_iota(jnp.int32, zb.shape, 1)
    bg_ref[...] = jnp.where(lane < GDN_HEADS, beta, decay)


def _unit_lower_inverses(ms):
    c = ms[0].shape[0]
    eye = (lax.broadcasted_iota(jnp.int32, (c, c), 0) == lax.broadcasted_iota(jnp.int32, (c, c), 1)).astype(F32)
    invs = [eye - m for m in ms]
    powers = [m.astype(BF16) for m in ms]
    width = 2
    while width < c:
        powers = [_dot(p, p).astype(BF16) for p in powers]
        invs = [inv + _dot(inv.astype(BF16), p) for inv, p in zip(invs, powers)]
        width *= 2
    return invs


def _gdn_chunk_kernel(q_ref, k_ref, v_ref, gate_ref, bg_ref, onorm_ref, out_ref, state_ref):
    c = GDN_CHUNK
    t = q_ref.shape[0]
    heads = range(GDN_HEADS)

    @pl.when(pl.program_id(1) == 0)
    def _():
        state_ref[...] = jnp.zeros(state_ref.shape, F32)

    rows = lax.broadcasted_iota(jnp.int32, (c, c), 0)
    cols = lax.broadcasted_iota(jnp.int32, (c, c), 1)
    tri = rows >= cols
    strict = rows > cols
    hi = functools.partial(jnp.dot, precision=lax.Precision.HIGHEST, preferred_element_type=F32)
    onorm = onorm_ref[...]
    for r0 in range(0, t, c):
        bg = bg_ref[r0:r0 + c, :]
        gcum = hi(tri.astype(F32), bg)
        gcum_t = gcum.T
        sl = [slice(hd * GDN_DK, (hd + 1) * GDN_DK) for hd in heads]
        q = [q_ref[r0:r0 + c, s].astype(F32) for s in sl]
        kbf = [k_ref[r0:r0 + c, s] for s in sl]
        k = [x.astype(F32) for x in kbf]
        v = [v_ref[r0:r0 + c, s].astype(F32) for s in sl]
        beta = [bg[:, hd:hd + 1] for hd in heads]
        gcol = [gcum[:, GDN_HEADS + hd:GDN_HEADS + hd + 1] for hd in heads]
        grow = [gcum_t[GDN_HEADS + hd:GDN_HEADS + hd + 1, :] for hd in heads]
        decay = [jnp.where(tri, jnp.exp(jnp.where(tri, gc - gr, 0.0)), 0.0) for gc, gr in zip(gcol, grow)]
        kb = [x * b for x, b in zip(k, beta)]
        kk = [lax.dot_general(a.astype(BF16), b, NT_DIMS, preferred_element_type=F32) for a, b in zip(kb, kbf)]
        attn = [lax.dot_general(a.astype(BF16), b, NT_DIMS, preferred_element_type=F32) * d
                for a, b, d in zip(q, kbf, decay)]
        t_inv = [x.astype(BF16) for x in
                 _unit_lower_inverses([jnp.where(strict, a * d, 0.0) for a, d in zip(kk, decay)])]
        eg = [jnp.exp(gc) for gc in gcol]
        u = [_dot(ti, (x * b).astype(BF16)) for ti, x, b in zip(t_inv, v, beta)]
        w = [_dot(ti, (x * e).astype(BF16)) for ti, x, e in zip(t_inv, kb, eg)]
        state = [state_ref[hd] for hd in heads]
        sb = [x.astype(BF16) for x in state]
        vnb = [(a - _dot(b.astype(BF16), s)).astype(BF16) for a, b, s in zip(u, w, sb)]
        o = [_dot((x * e).astype(BF16), s) + _dot(a.astype(BF16), vn)
             for x, e, s, a, vn in zip(q, eg, sb, attn, vnb)]
        g_last = [gc[c - 1:c, :] for gc in gcol]
        kd = [(x * jnp.exp(gl - gc)).astype(BF16) for x, gl, gc in zip(k, g_last, gcol)]
        for hd in heads:
            state_ref[hd] = state[hd] * jnp.exp(g_last[hd]) + lax.dot_general(
                kd[hd], vnb[hd], TN_DIMS, preferred_element_type=F32)
        gated = [(_rms(x, onorm) * _silu(gate_ref[r0:r0 + c, s].astype(F32))).astype(BF16)
                 for x, s in zip(o, sl)]
        out_ref[r0:r0 + c, :] = jnp.concatenate(gated, axis=-1)


def _gdn_mixer(x, g, w_in_x, conv_w, a_log, dt_bias, o_norm):
    b, s, d = x.shape
    hk = GDN_HEADS * GDN_DK
    lanes_pad = jnp.zeros((LANES - 2 * GDN_HEADS,), F32)
    zeros_h = jnp.zeros((GDN_HEADS,), F32)
    alog_row = jnp.concatenate([zeros_h, a_log, lanes_pad])[None, :]
    dtb_row = jnp.concatenate([zeros_h, dt_bias, lanes_pad])[None, :]
    tm = TM_GDN_PROJ
    row = lambda tt, c: pl.BlockSpec((None, tt, c), lambda bi, i: (bi, i, 0))
    vec = lambda c: pl.BlockSpec((1, c), lambda bi, i: (0, 0))
    q, k, v, gate, bg = pl.pallas_call(
        _gdn_proj_kernel,
        grid=(b, s // tm),
        in_specs=[row(tm, d), vec(d), _resident(w_in_x.shape),
                  pl.BlockSpec(conv_w.shape, lambda bi, i: (0, 0)), vec(LANES), vec(LANES)],
        out_specs=[row(tm, hk), row(tm, hk), row(tm, hk), row(tm, hk), row(tm, LANES)],
        out_shape=[jax.ShapeDtypeStruct((b, s, hk), BF16)] * 4 + [jax.ShapeDtypeStruct((b, s, LANES), F32)],
        scratch_shapes=[pltpu.VMEM((3, tm + SUBLANES, hk), F32)],
        compiler_params=_params("parallel", "arbitrary"),
        name="gdn_proj",
    )(x, g[None, :], w_in_x, conv_w, alog_row, dtb_row)

    tt = T_GDN
    return pl.pallas_call(
        _gdn_chunk_kernel,
        grid=(b, s // tt),
        in_specs=[row(tt, hk), row(tt, hk), row(tt, hk), row(tt, hk), row(tt, LANES), vec(GDN_DV)],
        out_specs=row(tt, hk),
        out_shape=jax.ShapeDtypeStruct((b, s, hk), BF16),
        scratch_shapes=[pltpu.VMEM((GDN_HEADS, GDN_DK, GDN_DV), F32)],
        compiler_params=_params("parallel", "arbitrary"),
        name="gdn_chunk",
    )(q, k, v, gate, bg, o_norm[None, :])


def kernel(x, mem, positions, mla_w_in, mla_q_norm, mla_kv_norm, mla_w_uq, mla_w_ukv, mla_w_o, gdn_w_in, gdn_conv_w, gdn_a_log, gdn_dt_bias, gdn_o_norm, gdn_w_o, sc_w_in, sc_conv_w, sc_w_o, norm_mix, norm_mem, norm_mlp, xa_w_q, xa_w_kv, xa_w_o, mlp_w1, mlp_w2, mem_norm, final_norm):
    b, s, d = x.shape
    depth = norm_mix.shape[0]
    m = mem.shape[1]
    cos_t, sin_t = _rope_tables(positions)
    kv_all = _mem_kv(mem.reshape(b * m, d), mem_norm, xa_w_kv.astype(BF16)).reshape(depth, b, m, 2 * d)
    for i in range(depth):
        j = i // N_MIXERS
        kind = i % N_MIXERS
        if kind == 0:
            w_in_x, w_uq_x, w_ukv_x = _mla_weights(mla_w_in[j], mla_w_uq[j], mla_w_ukv[j])
            pre = _mla_mixer(x, norm_mix[i], w_in_x, mla_q_norm[j], mla_kv_norm[j], w_uq_x, w_ukv_x,
                             cos_t, sin_t)
            w_mix = mla_w_o[j]
        elif kind == 1:
            pad = jnp.zeros((d, LANES - 2 * GDN_HEADS), F32)
            w_in_x = jnp.concatenate([gdn_w_in[j], pad], axis=1).astype(BF16)
            pre = _gdn_mixer(x, norm_mix[i], w_in_x, gdn_conv_w[j], gdn_a_log[j], gdn_dt_bias[j],
                             gdn_o_norm[j])
            w_mix = gdn_w_o[j]
        else:
            pre = _sc_mixer(x, norm_mix[i], sc_w_in[j].astype(BF16), sc_conv_w[j])
            w_mix = sc_w_o[j]
        x = _tail(x, pre, w_mix.astype(BF16), norm_mem[i], xa_w_q[i].astype(BF16), kv_all, i,
                  xa_w_o[i].astype(BF16))
        x = _mlp(x.reshape(b * s, d), norm_mlp[i], mlp_w1[i].astype(BF16), mlp_w2[i].astype(BF16),
                 final_norm, i == depth - 1).reshape(b, s, d)
    return x
```

```python
import functools

import jax
import jax.numpy as jnp
from jax import lax
from jax.experimental import pallas as pl
from jax.experimental.pallas import tpu as pltpu

F32 = jnp.float32
BF16 = jnp.bfloat16

EPS = 1e-6
ROPE_THETA = 10000.0
LOG2E = 1.4426950408889634
N_MIXERS = 3

MLA_HEADS = 8
MLA_NOPE = 128
MLA_ROPE = 64
MLA_V = 128
MLA_Q_RANK = 384
MLA_KV_RANK = 256

GDN_HEADS = 8
GDN_DK = 128
GDN_DV = 128
GDN_CONV = 4
GDN_CHUNK = 64

SC_CONV = 3
X_HEADS = 4

LANES = 128
SUBLANES = 8
VMEM_LIMIT_BYTES = 56 * 1024 * 1024

TM_TABLE = 1024
TM_MEM = 512
TM_TAIL = 512
TM_MLP = 512
TM_MLA_PROJ = 512
TQ_MLA = 256
TK_MLA = 256
TM_SC = 512
TM_GDN_PROJ = 256
T_GDN = 256
GDN_PROJ_COLS = 512
FF_CHUNK = 1024

NT_DIMS = (((1,), (1,)), ((), ()))
TN_DIMS = (((0,), (0,)), ((), ()))


def _params(*sem):
    return pltpu.CompilerParams(dimension_semantics=sem, vmem_limit_bytes=VMEM_LIMIT_BYTES)


def _resident(shape):
    nd = len(shape)
    return pl.BlockSpec(shape, lambda *_: (0,) * nd, pipeline_mode=pl.Buffered(1))


def _layer_resident(shape, layer):
    nd = len(shape)
    return pl.BlockSpec((None,) + shape, lambda *_: (layer,) + (0,) * nd, pipeline_mode=pl.Buffered(1))


def _rms(x, g):
    return x * lax.rsqrt(jnp.mean(x * x, axis=-1, keepdims=True) + EPS) * g


def _dot(a, b):
    return jnp.dot(a, b, preferred_element_type=F32)


def _silu(x):
    return x * jax.nn.sigmoid(x)


def _rope_table_kernel(pos_ref, freq_ref, cos_ref, sin_ref):
    ang = pos_ref[...] * freq_ref[...]
    cos_ref[...] = jnp.cos(ang)
    sin_ref[...] = jnp.sin(ang)


def _rope_tables(positions):
    n = positions.size
    pos = jnp.broadcast_to(positions.reshape(n, 1).astype(F32), (n, LANES))
    inv_freq = ROPE_THETA ** (-jnp.arange(0, MLA_ROPE, 2, dtype=F32) / MLA_ROPE)
    freq = jnp.tile(inv_freq, 2 * LANES // MLA_ROPE)[None, :]
    tm = TM_TABLE
    row = pl.BlockSpec((tm, LANES), lambda i: (i, 0))
    return pl.pallas_call(
        _rope_table_kernel,
        grid=(n // tm,),
        in_specs=[row, pl.BlockSpec((1, LANES), lambda i: (0, 0))],
        out_specs=[row, row],
        out_shape=[jax.ShapeDtypeStruct((n, LANES), F32)] * 2,
        compiler_params=_params("parallel"),
        name="rope_tables",
    )(pos, freq)


def _mem_kv_kernel(mem_ref, g_ref, w_ref, out_ref):
    mn = _rms(mem_ref[...], g_ref[...]).astype(BF16)
    out_ref[0] = _dot(mn, w_ref[0].astype(BF16)).astype(BF16)


def _mem_kv(mem2d, mem_norm, w_kv):
    n, d = mem2d.shape
    depth, _, d2 = w_kv.shape
    tm = TM_MEM
    return pl.pallas_call(
        _mem_kv_kernel,
        grid=(depth, n // tm),
        in_specs=[
            pl.BlockSpec((tm, d), lambda l, i: (i, 0)),
            pl.BlockSpec((1, d), lambda l, i: (0, 0)),
            pl.BlockSpec((1, d, d2), lambda l, i: (l, 0, 0)),
        ],
        out_specs=pl.BlockSpec((1, tm, d2), lambda l, i: (l, i, 0)),
        out_shape=jax.ShapeDtypeStruct((depth, n, d2), BF16),
        compiler_params=_params("parallel", "parallel"),
        name="mem_kv",
    )(mem2d, mem_norm[None, :], w_kv)


def _tail_kernel(x_ref, pre_ref, wmix_ref, g_ref, wq_ref, kv_ref, wo_ref, out_ref):
    d = x_ref.shape[-1]
    dh = d // X_HEADS
    x = x_ref[...] + _dot(pre_ref[...], wmix_ref[...].astype(BF16))
    h = _rms(x, g_ref[...]).astype(BF16)
    q = (_dot(h, wq_ref[...].astype(BF16)) * (dh ** -0.5 * LOG2E)).astype(BF16)
    scores = [lax.dot_general(q[:, hd * dh:(hd + 1) * dh], kv_ref[:, hd * dh:(hd + 1) * dh], NT_DIMS,
                              preferred_element_type=F32) for hd in range(X_HEADS)]
    heads = []
    for hd, s in enumerate(scores):
        p = jnp.exp2(s - jnp.max(s, axis=-1, keepdims=True))
        l = jnp.sum(p, axis=-1, keepdims=True)
        heads.append((_dot(p.astype(BF16), kv_ref[:, d + hd * dh:d + (hd + 1) * dh]) / l).astype(BF16))
    o = jnp.concatenate(heads, axis=-1)
    out_ref[...] = x + _dot(o, wo_ref[...].astype(BF16))


def _tail(x, pre, w_mix, mix_layer, g_mem, w_q, kv_all, layer, w_o):
    b, s, d = x.shape
    m = kv_all.shape[2]
    tm = TM_TAIL
    row = pl.BlockSpec((None, tm, d), lambda bi, i: (bi, i, 0))
    return pl.pallas_call(
        _tail_kernel,
        grid=(b, s // tm),
        in_specs=[
            row, row, _layer_resident((d, d), mix_layer),
            pl.BlockSpec((1, d), lambda bi, i: (0, 0)),
            _layer_resident((d, d), layer),
            pl.BlockSpec((None, None, m, 2 * d), lambda bi, i: (layer, bi, 0, 0)),
            _layer_resident((d, d), layer),
        ],
        out_specs=row,
        out_shape=jax.ShapeDtypeStruct((b, s, d), F32),
        compiler_params=_params("parallel", "parallel"),
        name="tail",
    )(x, pre, w_mix, g_mem[None, :], w_q, kv_all, w_o)


def _mlp_kernel(x_ref, g_ref, w1_ref, w2_ref, gf_ref, out_ref, *, final):
    x = x_ref[...]
    h = _rms(x, g_ref[...]).astype(BF16)
    acc = x
    for c in range(0, w1_ref.shape[1], FF_CHUNK):
        a = jnp.maximum(_dot(h, w1_ref[:, c:c + FF_CHUNK].astype(BF16)), 0.0)
        acc = acc + _dot((a * a).astype(BF16), w2_ref[c:c + FF_CHUNK, :].astype(BF16))
    if final:
        acc = _rms(acc, gf_ref[...])
    out_ref[...] = acc


def _mlp(x2d, g, w1, w2, layer, g_final, final):
    n, d = x2d.shape
    ff = w1.shape[2]
    tm = TM_MLP
    row = pl.BlockSpec((tm, d), lambda i: (i, 0))
    vec = pl.BlockSpec((1, d), lambda i: (0, 0))
    return pl.pallas_call(
        functools.partial(_mlp_kernel, final=final),
        grid=(n // tm,),
        in_specs=[row, vec, _layer_resident((d, ff), layer), _layer_resident((ff, d), layer), vec],
        out_specs=row,
        out_shape=jax.ShapeDtypeStruct((n, d), F32),
        compiler_params=_params("parallel"),
        name="mlp",
    )(x2d, g[None, :], w1, w2, g_final[None, :])


def _mla_weights(w_in, w_uq, w_ukv):
    half = MLA_ROPE // 2

    def swap_cols(w):
        return jnp.concatenate([-w[..., half:], w[..., :half]], axis=-1)

    lat = MLA_Q_RANK + MLA_KV_RANK
    kr = w_in[:, lat:]
    w_in_x = jnp.concatenate([w_in[:, :lat], kr, kr, swap_cols(kr), swap_cols(kr)], axis=1)
    uq = w_uq.reshape(MLA_Q_RANK, MLA_HEADS, MLA_NOPE + MLA_ROPE)
    qn = uq[:, :, :MLA_NOPE].reshape(MLA_Q_RANK, -1)
    qr = uq[:, :, MLA_NOPE:]
    w_uq_x = jnp.concatenate(
        [qn, qr.reshape(MLA_Q_RANK, -1), swap_cols(qr).reshape(MLA_Q_RANK, -1)], axis=1)
    ukv = w_ukv.reshape(MLA_KV_RANK, MLA_HEADS, MLA_NOPE + MLA_V)
    w_kn_t = ukv[:, :, :MLA_NOPE].reshape(MLA_KV_RANK, -1).T
    w_v = ukv[:, :, MLA_NOPE:].reshape(MLA_KV_RANK, -1)
    return w_in_x.astype(BF16), w_uq_x.astype(BF16), w_kn_t.astype(BF16), w_v.astype(BF16)


def _mla_proj_kernel(x_ref, g_ref, win_ref, gq_ref, gkv_ref, wuq_ref, wknt_ref, wv_ref, cos_ref, sin_ref,
                     qn_ref, qr_ref, knt_ref, krt_ref, v_ref):
    hn = MLA_HEADS * MLA_NOPE
    hr = MLA_HEADS * MLA_ROPE
    lat = MLA_Q_RANK + MLA_KV_RANK
    scale = (MLA_NOPE + MLA_ROPE) ** -0.5 * LOG2E
    h = _rms(x_ref[...], g_ref[...]).astype(BF16)
    z = _dot(h, win_ref[...])
    cos = cos_ref[...]
    sin = sin_ref[...]
    kr_t = (z[:, lat:lat + LANES] * cos + z[:, lat + LANES:lat + 2 * LANES] * sin).T
    low = lax.broadcasted_iota(jnp.int32, kr_t.shape, 0) < MLA_ROPE
    krt_ref[0:LANES, :] = jnp.where(low, kr_t, 0.0).astype(BF16)
    krt_ref[LANES:2 * LANES, :] = jnp.where(low, 0.0, kr_t).astype(BF16)
    cq = _rms(z[:, :MLA_Q_RANK], gq_ref[...]).astype(BF16)
    ckv = _rms(z[:, MLA_Q_RANK:lat], gkv_ref[...])
    q = _dot(cq, wuq_ref[...]) * scale
    qn_ref[...] = q[:, :hn].astype(BF16)
    reps = hr // LANES
    cos_r = jnp.concatenate([cos] * reps, axis=-1)
    sin_r = jnp.concatenate([sin] * reps, axis=-1)
    qr_ref[...] = (q[:, hn:hn + hr] * cos_r + q[:, hn + hr:hn + 2 * hr] * sin_r).astype(BF16)
    knt_ref[...] = _dot(wknt_ref[...], ckv.T.astype(BF16)).astype(BF16)
    v_ref[...] = _dot(ckv.astype(BF16), wv_ref[...]).astype(BF16)


def _mla_attn_kernel(qn_ref, qr_ref, knt_ref, krt_ref, v_ref, o_ref):
    s_len = qn_ref.shape[0]
    tq = TQ_MLA
    tk = TK_MLA
    rows = lax.broadcasted_iota(jnp.int32, (tq, tk), 0)
    cols = lax.broadcasted_iota(jnp.int32, (tq, tk), 1)

    def scores(q0):
        q1 = q0 + tq
        d0 = q0 // tk * tk
        q = jnp.concatenate([qn_ref[q0:q1, :], qr_ref[q0:q1, :]], axis=-1)
        k_diag = jnp.concatenate([knt_ref[:, d0:d0 + tk], krt_ref[:, d0:d0 + tk]], axis=0)
        s_diag = jnp.where(rows + (q0 - d0) >= cols, _dot(q, k_diag), -jnp.inf)
        if not d0:
            return s_diag, None
        k_past = jnp.concatenate([knt_ref[:, 0:d0], krt_ref[:, 0:d0]], axis=0)
        return s_diag, _dot(q, k_past)

    def softmax(q0, s_diag, s_past):
        m = jnp.max(s_diag, axis=-1, keepdims=True)
        if s_past is not None:
            m = jnp.maximum(m, jnp.max(s_past, axis=-1, keepdims=True))
        p_diag = jnp.exp2(s_diag - m)
        l = jnp.sum(p_diag, axis=-1, keepdims=True)
        p_past = None
        if s_past is not None:
            p_past = jnp.exp2(s_past - m)
            l = l + jnp.sum(p_past, axis=-1, keepdims=True)
            p_past = p_past.astype(BF16)
        return q0, p_diag.astype(BF16), p_past, l

    def values(q0, p_diag, p_past, l):
        d0 = q0 // tk * tk
        acc = _dot(p_diag, v_ref[d0:d0 + tk, :])
        if p_past is not None:
            acc = acc + _dot(p_past, v_ref[0:d0, :])
        o_ref[q0:q0 + tq, :] = (acc / l).astype(BF16)

    starts = list(range(0, s_len, tq))
    nxt = scores(starts[0])
    pending = None
    for idx, q0 in enumerate(starts):
        cur = nxt
        if idx + 1 < len(starts):
            nxt = scores(starts[idx + 1])
        if pending is not None:
            values(*pending)
        pending = softmax(q0, *cur)
    values(*pending)


def _mla_mixer(x, g, w_in_x, gq, gkv, w_uq_x, w_kn_t, w_v, cos_t, sin_t):
    b, s, d = x.shape
    hn = MLA_HEADS * MLA_NOPE
    hr = MLA_HEADS * MLA_ROPE
    tm = TM_MLA_PROJ
    row = lambda w: pl.BlockSpec((None, tm, w), lambda bi, i: (bi, i, 0))
    col = lambda r: pl.BlockSpec((None, r, tm), lambda bi, i: (bi, 0, i))
    vec = lambda w: pl.BlockSpec((1, w), lambda bi, i: (0, 0))
    qn, qr, knt, krt, v = pl.pallas_call(
        _mla_proj_kernel,
        grid=(b, s // tm),
        in_specs=[row(d), vec(d), _resident(w_in_x.shape), vec(MLA_Q_RANK), vec(MLA_KV_RANK),
                  _resident(w_uq_x.shape), _resident(w_kn_t.shape), _resident(w_v.shape),
                  row(LANES), row(LANES)],
        out_specs=[row(hn), row(hr), col(hn), col(2 * LANES), row(hn)],
        out_shape=[jax.ShapeDtypeStruct((b, s, hn), BF16), jax.ShapeDtypeStruct((b, s, hr), BF16),
                   jax.ShapeDtypeStruct((b, hn, s), BF16), jax.ShapeDtypeStruct((b, 2 * LANES, s), BF16),
                   jax.ShapeDtypeStruct((b, s, hn), BF16)],
        compiler_params=_params("parallel", "parallel"),
        name="mla_proj",
    )(x, g[None, :], w_in_x, gq[None, :], gkv[None, :], w_uq_x, w_kn_t, w_v,
      cos_t.reshape(b, s, LANES), sin_t.reshape(b, s, LANES))

    head = pl.BlockSpec((None, s, LANES), lambda bi, hd: (bi, 0, hd))
    return pl.pallas_call(
        _mla_attn_kernel,
        grid=(b, MLA_HEADS),
        in_specs=[head,
                  pl.BlockSpec((None, s, LANES), lambda bi, hd: (bi, 0, hd // 2)),
                  pl.BlockSpec((None, LANES, s), lambda bi, hd: (bi, hd, 0)),
                  pl.BlockSpec((None, LANES, s), lambda bi, hd: (bi, hd % 2, 0)),
                  head],
        out_specs=head,
        out_shape=jax.ShapeDtypeStruct((b, s, hn), BF16),
        compiler_params=_params("parallel", "parallel"),
        name="mla_attn",
    )(qn, qr, knt, krt, v)


def _causal_conv(z, w_ref, col, carry_ref, first):
    tm, c = z.shape
    k = w_ref.shape[0]

    @pl.when(first)
    def _():
        carry_ref[...] = jnp.zeros((SUBLANES, c), F32)

    head = jnp.concatenate([carry_ref[...], z[0:SUBLANES, :]], axis=0)
    y = w_ref[k - 1:k, col:col + c] * z
    y_head = y[0:SUBLANES, :]
    for j in range(1, k):
        tap = w_ref[k - 1 - j:k - j, col:col + c]
        y = y + tap * pltpu.roll(z, j, 0)
        y_head = y_head + tap * pltpu.roll(head, j, 0)[SUBLANES:, :]
    carry_ref[...] = z[tm - SUBLANES:, :]
    return jnp.concatenate([y_head, y[SUBLANES:, :]], axis=0)


def _sc_kernel(x_ref, g_ref, win_ref, cw_ref, out_ref, buf_ref):
    w = out_ref.shape[-1]
    h = _rms(x_ref[...], g_ref[...]).astype(BF16)
    cu = _dot(h, win_ref[:, w:2 * w].astype(BF16)) * _dot(h, win_ref[:, 2 * w:3 * w].astype(BF16))
    y = _causal_conv(cu, cw_ref, 0, buf_ref, pl.program_id(1) == 0)
    out_ref[...] = (_dot(h, win_ref[:, 0:w].astype(BF16)) * y).astype(BF16)


def _sc_mixer(x, g, w_in, layer, conv_w):
    b, s, d = x.shape
    w = conv_w.shape[1]
    tm = TM_SC
    row = lambda c: pl.BlockSpec((None, tm, c), lambda bi, i: (bi, i, 0))
    return pl.pallas_call(
        _sc_kernel,
        grid=(b, s // tm),
        in_specs=[row(d), pl.BlockSpec((1, d), lambda bi, i: (0, 0)),
                  _layer_resident(w_in.shape[1:], layer),
                  pl.BlockSpec(conv_w.shape, lambda bi, i: (0, 0))],
        out_specs=row(w),
        out_shape=jax.ShapeDtypeStruct((b, s, w), BF16),
        scratch_shapes=[pltpu.VMEM((SUBLANES, w), F32)],
        compiler_params=_params("parallel", "arbitrary"),
        name="short_conv",
    )(x, g[None, :], w_in, conv_w)


def _gdn_proj_kernel(x_ref, g_ref, win_ref, cw_ref, alog_ref, dtb_ref,
                     q_ref, k_ref, v_ref, gate_ref, bg_ref, buf_ref):
    hk = GDN_HEADS * GDN_DK
    first = pl.program_id(1) == 0
    h = _rms(x_ref[...], g_ref[...]).astype(BF16)
    cb = GDN_PROJ_COLS
    proj = lambda c0: _dot(h, win_ref[:, c0:c0 + cb])
    gate_cols = list(range(0, hk, cb))
    z_next = proj(0)
    for c0 in range(0, 3 * hk, cb):
        z = z_next
        if c0 + cb < 3 * hk:
            z_next = proj(c0 + cb)
        if gate_cols:
            g0 = gate_cols.pop(0)
            gate_ref[:, g0:g0 + cb] = proj(3 * hk + g0).astype(BF16)
        part, off = divmod(c0, hk)
        ref = (q_ref, k_ref, v_ref)[part]
        y = _silu(_causal_conv(z, cw_ref, c0, buf_ref.at[:, c0:c0 + cb], first))
        if part < 2:
            post = GDN_DK ** -0.5 if part == 0 else 1.0
            for d0 in range(0, cb, GDN_DK):
                yh = y[:, d0:d0 + GDN_DK]
                inv = lax.rsqrt(jnp.sum(yh * yh, axis=-1, keepdims=True) + EPS) * post
                ref[:, off + d0:off + d0 + GDN_DK] = (yh * inv).astype(BF16)
        else:
            ref[:, off:off + cb] = y.astype(BF16)
    for g0 in gate_cols:
        gate_ref[:, g0:g0 + cb] = proj(3 * hk + g0).astype(BF16)
    zb = _dot(h, win_ref[:, 4 * hk:4 * hk + LANES])
    beta = jax.nn.sigmoid(zb)
    t = zb + dtb_ref[...]
    softplus = jnp.maximum(t, 0.0) + jnp.log1p(jnp.exp(-jnp.abs(t)))
    decay = -jnp.exp(alog_ref[...]) * softplus
    lane = lax.broadcasted_iota(jnp.int32, zb.shape, 1)
    bg_ref[...] = jnp.where(lane < GDN_HEADS, beta, decay)


def _unit_lower_inverses(ms):
    c = ms[0].shape[0]
    eye = (lax.broadcasted_iota(jnp.int32, (c, c), 0) == lax.broadcasted_iota(jnp.int32, (c, c), 1)).astype(F32)
    invs = [eye - m for m in ms]
    powers = [m.astype(BF16) for m in ms]
    width = 2
    while width < c:
        powers = [_dot(p, p).astype(BF16) for p in powers]
        invs = [inv + _dot(inv.astype(BF16), p) for inv, p in zip(invs, powers)]
        width *= 2
    return invs


def _gdn_chunk_kernel(q_ref, k_ref, v_ref, gate_ref, bg_ref, onorm_ref, out_ref, state_ref):
    c = GDN_CHUNK
    t = q_ref.shape[0]
    heads = range(GDN_HEADS)

    @pl.when(pl.program_id(1) == 0)
    def _():
        state_ref[...] = jnp.zeros(state_ref.shape, F32)

    rows = lax.broadcasted_iota(jnp.int32, (c, c), 0)
    cols = lax.broadcasted_iota(jnp.int32, (c, c), 1)
    tri = rows >= cols
    strict = rows > cols
    hi = functools.partial(jnp.dot, precision=lax.Precision.HIGHEST, preferred_element_type=F32)
    onorm = onorm_ref[...]
    sl = [slice(hd * GDN_DK, (hd + 1) * GDN_DK) for hd in heads]

    def intra(r0s):
        pairs = [(r0, hd) for r0 in r0s for hd in heads]
        bgs = {r0: bg_ref[r0:r0 + c, :] for r0 in r0s}
        gcums = {r0: hi(tri.astype(F32), bgs[r0]) for r0 in r0s}
        gcums_t = {r0: gcums[r0].T for r0 in r0s}
        q = [q_ref[r0:r0 + c, sl[hd]].astype(F32) for r0, hd in pairs]
        kbf = [k_ref[r0:r0 + c, sl[hd]] for r0, hd in pairs]
        k = [x.astype(F32) for x in kbf]
        v = [v_ref[r0:r0 + c, sl[hd]].astype(F32) for r0, hd in pairs]
        beta = [bgs[r0][:, hd:hd + 1] for r0, hd in pairs]
        gcol = [gcums[r0][:, GDN_HEADS + hd:GDN_HEADS + hd + 1] for r0, hd in pairs]
        grow = [gcums_t[r0][GDN_HEADS + hd:GDN_HEADS + hd + 1, :] for r0, hd in pairs]
        decay = [jnp.where(tri, jnp.exp(jnp.where(tri, gc - gr, 0.0)), 0.0) for gc, gr in zip(gcol, grow)]
        kb = [x * b for x, b in zip(k, beta)]
        kk = [lax.dot_general(a.astype(BF16), b, NT_DIMS, preferred_element_type=F32) for a, b in zip(kb, kbf)]
        attn = [lax.dot_general(a.astype(BF16), b, NT_DIMS, preferred_element_type=F32) * d
                for a, b, d in zip(q, kbf, decay)]
        t_inv = [x.astype(BF16) for x in
                 _unit_lower_inverses([jnp.where(strict, a * d, 0.0) for a, d in zip(kk, decay)])]
        eg = [jnp.exp(gc) for gc in gcol]
        u = [_dot(ti, (x * b).astype(BF16)) for ti, x, b in zip(t_inv, v, beta)]
        w = [_dot(ti, (x * e).astype(BF16)).astype(BF16) for ti, x, e in zip(t_inv, kb, eg)]
        qe = [(x * e).astype(BF16) for x, e in zip(q, eg)]
        g_last = [gc[c - 1:c, :] for gc in gcol]
        kd = [(x * jnp.exp(gl - gc)).astype(BF16) for x, gl, gc in zip(k, g_last, gcol)]
        e_last = [jnp.exp(gl) for gl in g_last]
        fields = (u, w, [a.astype(BF16) for a in attn], qe, kd, e_last)
        n = len(heads)
        return [tuple(f[i * n:(i + 1) * n] for f in fields) for i in range(len(r0s))]

    def recur(r0, pre, state):
        u, w, attn, qe, kd, e_last = pre
        sb = [x.astype(BF16) for x in state]
        vnb = [(a - _dot(b, s)).astype(BF16) for a, b, s in zip(u, w, sb)]
        o = [_dot(x, s) + _dot(a, vn) for x, s, a, vn in zip(qe, sb, attn, vnb)]
        state = [s * e + lax.dot_general(x, vn, TN_DIMS, preferred_element_type=F32)
                 for s, e, x, vn in zip(state, e_last, kd, vnb)]
        gated = [(_rms(x, onorm) * _silu(gate_ref[r0:r0 + c, s].astype(F32))).astype(BF16)
                 for x, s in zip(o, sl)]
        out_ref[r0:r0 + c, :] = jnp.concatenate(gated, axis=-1)
        return state

    starts = list(range(0, t, c))
    state = [state_ref[hd] for hd in heads]
    for r0, pre in zip(starts, intra(starts)):
        state = recur(r0, pre, state)
    for hd in heads:
        state_ref[hd] = state[hd]


def _gdn_mixer(x, g, w_in_x, conv_w, a_log, dt_bias, o_norm):
    b, s, d = x.shape
    hk = GDN_HEADS * GDN_DK
    lanes_pad = jnp.zeros((LANES - 2 * GDN_HEADS,), F32)
    zeros_h = jnp.zeros((GDN_HEADS,), F32)
    alog_row = jnp.concatenate([zeros_h, a_log, lanes_pad])[None, :]
    dtb_row = jnp.concatenate([zeros_h, dt_bias, lanes_pad])[None, :]
    tm = TM_GDN_PROJ
    row = lambda tt, c: pl.BlockSpec((None, tt, c), lambda bi, i: (bi, i, 0))
    vec = lambda c: pl.BlockSpec((1, c), lambda bi, i: (0, 0))
    q, k, v, gate, bg = pl.pallas_call(
        _gdn_proj_kernel,
        grid=(b, s // tm),
        in_specs=[row(tm, d), vec(d), _resident(w_in_x.shape),
                  pl.BlockSpec(conv_w.shape, lambda bi, i: (0, 0)), vec(LANES), vec(LANES)],
        out_specs=[row(tm, hk), row(tm, hk), row(tm, hk), row(tm, hk), row(tm, LANES)],
        out_shape=[jax.ShapeDtypeStruct((b, s, hk), BF16)] * 4 + [jax.ShapeDtypeStruct((b, s, LANES), F32)],
        scratch_shapes=[pltpu.VMEM((SUBLANES, 3 * hk), F32)],
        compiler_params=_params("parallel", "arbitrary"),
        name="gdn_proj",
    )(x, g[None, :], w_in_x, conv_w, alog_row, dtb_row)

    tt = T_GDN
    return pl.pallas_call(
        _gdn_chunk_kernel,
        grid=(b, s // tt),
        in_specs=[row(tt, hk), row(tt, hk), row(tt, hk), row(tt, hk), row(tt, LANES), vec(GDN_DV)],
        out_specs=row(tt, hk),
        out_shape=jax.ShapeDtypeStruct((b, s, hk), BF16),
        scratch_shapes=[pltpu.VMEM((GDN_HEADS, GDN_DK, GDN_DV), F32)],
        compiler_params=_params("parallel", "arbitrary"),
        name="gdn_chunk",
    )(q, k, v, gate, bg, o_norm[None, :])


def kernel(x, mem, positions, mla_w_in, mla_q_norm, mla_kv_norm, mla_w_uq, mla_w_ukv, mla_w_o, gdn_w_in, gdn_conv_w, gdn_a_log, gdn_dt_bias, gdn_o_norm, gdn_w_o, sc_w_in, sc_conv_w, sc_w_o, norm_mix, norm_mem, norm_mlp, xa_w_q, xa_w_kv, xa_w_o, mlp_w1, mlp_w2, mem_norm, final_norm):
    b, s, d = x.shape
    depth = norm_mix.shape[0]
    m = mem.shape[1]
    cos_t, sin_t = _rope_tables(positions)
    kv_all = _mem_kv(mem.reshape(b * m, d), mem_norm, xa_w_kv).reshape(depth, b, m, 2 * d)
    for i in range(depth):
        j = i // N_MIXERS
        kind = i % N_MIXERS
        if kind == 0:
            w_in_x, w_uq_x, w_kn_t, w_v = _mla_weights(mla_w_in[j], mla_w_uq[j], mla_w_ukv[j])
            pre = _mla_mixer(x, norm_mix[i], w_in_x, mla_q_norm[j], mla_kv_norm[j], w_uq_x, w_kn_t, w_v,
                             cos_t, sin_t)
            w_mix = mla_w_o
        elif kind == 1:
            pad = jnp.zeros((d, LANES - 2 * GDN_HEADS), F32)
            w_in_x = jnp.concatenate([gdn_w_in[j], pad], axis=1).astype(BF16)
            pre = _gdn_mixer(x, norm_mix[i], w_in_x, gdn_conv_w[j], gdn_a_log[j], gdn_dt_bias[j],
                             gdn_o_norm[j])
            w_mix = gdn_w_o
        else:
            pre = _sc_mixer(x, norm_mix[i], sc_w_in, j, sc_conv_w[j])
            w_mix = sc_w_o
        x = _tail(x, pre, w_mix, j, norm_mem[i], xa_w_q, kv_all, i, xa_w_o)
        x = _mlp(x.reshape(b * s, d), norm_mlp[i], mlp_w1, mlp_w2, i,
                 final_norm, i == depth - 1).reshape(b, s, d)
    return x
```

```python
import functools

import jax
import jax.numpy as jnp
from jax import lax
from jax.experimental import pallas as pl
from jax.experimental.pallas import tpu as pltpu

F32 = jnp.float32
BF16 = jnp.bfloat16

EPS = 1e-6
ROPE_THETA = 10000.0
LOG2E = 1.4426950408889634
N_MIXERS = 3

MLA_HEADS = 8
MLA_NOPE = 128
MLA_ROPE = 64
MLA_V = 128
MLA_Q_RANK = 384
MLA_KV_RANK = 256

GDN_HEADS = 8
GDN_DK = 128
GDN_DV = 128
GDN_CONV = 4
GDN_CHUNK = 64

SC_CONV = 3
X_HEADS = 4

LANES = 128
SUBLANES = 8
VMEM_LIMIT_BYTES = 56 * 1024 * 1024

TM_TABLE = 1024
TM_MEM = 1024
TM_TAIL = 512
TM_MLP = 512
TM_MLA_PROJ = 512
TQ_MLA = 256
TK_MLA = 256
TM_SC = 512
TM_GDN_PROJ = 256
T_GDN = 128
GDN_SEQS = 2
GDN_PROJ_COLS = 256
SC_COLS = 256
FF_CHUNK = 1024

NT_DIMS = (((1,), (1,)), ((), ()))
TN_DIMS = (((0,), (0,)), ((), ()))


def _params(*sem):
    return pltpu.CompilerParams(dimension_semantics=sem, vmem_limit_bytes=VMEM_LIMIT_BYTES)


def _resident(shape):
    nd = len(shape)
    return pl.BlockSpec(shape, lambda *_: (0,) * nd, pipeline_mode=pl.Buffered(1))


def _layer_resident(shape, layer):
    nd = len(shape)
    return pl.BlockSpec((None,) + shape, lambda *_: (layer,) + (0,) * nd, pipeline_mode=pl.Buffered(1))


def _rms(x, g):
    return x * lax.rsqrt(jnp.mean(x * x, axis=-1, keepdims=True) + EPS) * g


def _dot(a, b):
    return jnp.dot(a, b, preferred_element_type=F32)


def _sigmoid(x):
    return 1.0 / (1.0 + jnp.exp2(x * -LOG2E))


def _silu(x):
    return x * _sigmoid(x)


def _rope_table_kernel(pos_ref, freq_ref, cos_ref, sin_ref):
    ang = pos_ref[...] * freq_ref[...]
    cos_ref[...] = jnp.cos(ang)
    sin_ref[...] = jnp.sin(ang)


def _rope_tables(positions):
    n = positions.size
    pos = jnp.broadcast_to(positions.reshape(n, 1).astype(F32), (n, LANES))
    inv_freq = ROPE_THETA ** (-jnp.arange(0, MLA_ROPE, 2, dtype=F32) / MLA_ROPE)
    freq = jnp.tile(inv_freq, 2 * LANES // MLA_ROPE)[None, :]
    tm = TM_TABLE
    row = pl.BlockSpec((tm, LANES), lambda i: (i, 0))
    return pl.pallas_call(
        _rope_table_kernel,
        grid=(n // tm,),
        in_specs=[row, pl.BlockSpec((1, LANES), lambda i: (0, 0))],
        out_specs=[row, row],
        out_shape=[jax.ShapeDtypeStruct((n, LANES), F32)] * 2,
        compiler_params=_params("parallel"),
        name="rope_tables",
    )(pos, freq)


def _mem_kv_kernel(mem_ref, g_ref, w_ref, out_ref):
    mn = _rms(mem_ref[...], g_ref[...]).astype(BF16)
    out_ref[0] = _dot(mn, w_ref[0].astype(BF16)).astype(BF16)


def _mem_kv(mem2d, mem_norm, w_kv):
    n, d = mem2d.shape
    depth, _, d2 = w_kv.shape
    tm = min(TM_MEM, n)
    return pl.pallas_call(
        _mem_kv_kernel,
        grid=(depth, n // tm),
        in_specs=[
            pl.BlockSpec((tm, d), lambda l, i: (i, 0)),
            pl.BlockSpec((1, d), lambda l, i: (0, 0)),
            pl.BlockSpec((1, d, d2), lambda l, i: (l, 0, 0)),
        ],
        out_specs=pl.BlockSpec((1, tm, d2), lambda l, i: (l, i, 0)),
        out_shape=jax.ShapeDtypeStruct((depth, n, d2), BF16),
        compiler_params=_params("parallel", "parallel"),
        name="mem_kv",
    )(mem2d, mem_norm[None, :], w_kv)


def _tail_kernel(x_ref, pre_ref, wmix_ref, g_ref, wq_ref, kv_ref, wo_ref, out_ref):
    d = x_ref.shape[-1]
    dh = d // X_HEADS
    x = x_ref[...] + _dot(pre_ref[...], wmix_ref[...].astype(BF16))
    h = _rms(x, g_ref[...]).astype(BF16)
    q = (_dot(h, wq_ref[...].astype(BF16)) * (dh ** -0.5 * LOG2E)).astype(BF16)
    scores = [lax.dot_general(q[:, hd * dh:(hd + 1) * dh], kv_ref[:, hd * dh:(hd + 1) * dh], NT_DIMS,
                              preferred_element_type=F32) for hd in range(X_HEADS)]
    heads = []
    for hd, s in enumerate(scores):
        p = jnp.exp2(s - jnp.max(s, axis=-1, keepdims=True))
        l = jnp.sum(p, axis=-1, keepdims=True)
        heads.append((_dot(p.astype(BF16), kv_ref[:, d + hd * dh:d + (hd + 1) * dh]) / l).astype(BF16))
    o = jnp.concatenate(heads, axis=-1)
    out_ref[...] = x + _dot(o, wo_ref[...].astype(BF16))


def _tail(x, pre, w_mix, mix_layer, g_mem, w_q, kv_all, layer, w_o):
    b, s, d = x.shape
    m = kv_all.shape[2]
    tm = TM_TAIL
    row = pl.BlockSpec((None, tm, d), lambda bi, i: (bi, i, 0))
    return pl.pallas_call(
        _tail_kernel,
        grid=(b, s // tm),
        in_specs=[
            row, row, _layer_resident((d, d), mix_layer),
            pl.BlockSpec((1, d), lambda bi, i: (0, 0)),
            _layer_resident((d, d), layer),
            pl.BlockSpec((None, None, m, 2 * d), lambda bi, i: (layer, bi, 0, 0)),
            _layer_resident((d, d), layer),
        ],
        out_specs=row,
        out_shape=jax.ShapeDtypeStruct((b, s, d), F32),
        compiler_params=_params("parallel", "parallel"),
        name="tail",
    )(x, pre, w_mix, g_mem[None, :], w_q, kv_all, w_o)


def _mlp_kernel(x_ref, g_ref, w1_ref, w2_ref, gf_ref, out_ref, *, final):
    x = x_ref[...]
    h = _rms(x, g_ref[...]).astype(BF16)
    acc = x
    up = lambda c: _dot(h, w1_ref[:, c:c + FF_CHUNK].astype(BF16))
    ff = w1_ref.shape[1]
    a_next = up(0)
    for c in range(0, ff, FF_CHUNK):
        a = jnp.maximum(a_next, 0.0)
        if c + FF_CHUNK < ff:
            a_next = up(c + FF_CHUNK)
        acc = acc + _dot((a * a).astype(BF16), w2_ref[c:c + FF_CHUNK, :].astype(BF16))
    if final:
        acc = _rms(acc, gf_ref[...])
    out_ref[...] = acc


def _mlp(x2d, g, w1, w2, layer, g_final, final):
    n, d = x2d.shape
    ff = w1.shape[2]
    tm = TM_MLP
    row = pl.BlockSpec((tm, d), lambda i: (i, 0))
    vec = pl.BlockSpec((1, d), lambda i: (0, 0))
    return pl.pallas_call(
        functools.partial(_mlp_kernel, final=final),
        grid=(n // tm,),
        in_specs=[row, vec, _layer_resident((d, ff), layer), _layer_resident((ff, d), layer), vec],
        out_specs=row,
        out_shape=jax.ShapeDtypeStruct((n, d), F32),
        compiler_params=_params("parallel"),
        name="mlp",
    )(x2d, g[None, :], w1, w2, g_final[None, :])


def _mla_weights(w_in, w_uq, w_ukv):
    half = MLA_ROPE // 2

    def swap_cols(w):
        return jnp.concatenate([-w[..., half:], w[..., :half]], axis=-1)

    lat = MLA_Q_RANK + MLA_KV_RANK
    kr = w_in[:, lat:]
    w_in_x = jnp.concatenate([w_in[:, :lat], kr, kr, swap_cols(kr), swap_cols(kr)], axis=1)
    uq = w_uq.reshape(MLA_Q_RANK, MLA_HEADS, MLA_NOPE + MLA_ROPE)
    qn = uq[:, :, :MLA_NOPE].reshape(MLA_Q_RANK, -1)
    qr = uq[:, :, MLA_NOPE:]
    w_uq_x = jnp.concatenate(
        [qn, qr.reshape(MLA_Q_RANK, -1), swap_cols(qr).reshape(MLA_Q_RANK, -1)], axis=1)
    ukv = w_ukv.reshape(MLA_KV_RANK, MLA_HEADS, MLA_NOPE + MLA_V)
    w_kn_t = ukv[:, :, :MLA_NOPE].reshape(MLA_KV_RANK, -1).T
    w_v = ukv[:, :, MLA_NOPE:].reshape(MLA_KV_RANK, -1)
    return w_in_x.astype(BF16), w_uq_x.astype(BF16), w_kn_t.astype(BF16), w_v.astype(BF16)


def _mla_proj_kernel(x_ref, g_ref, win_ref, gq_ref, gkv_ref, wuq_ref, wknt_ref, wv_ref, cos_ref, sin_ref,
                     qn_ref, qr_ref, knt_ref, krt_ref, v_ref):
    hn = MLA_HEADS * MLA_NOPE
    hr = MLA_HEADS * MLA_ROPE
    lat = MLA_Q_RANK + MLA_KV_RANK
    scale = (MLA_NOPE + MLA_ROPE) ** -0.5 * LOG2E
    h = _rms(x_ref[...], g_ref[...]).astype(BF16)
    z = _dot(h, win_ref[...])
    cos = cos_ref[...]
    sin = sin_ref[...]
    kr_t = (z[:, lat:lat + LANES] * cos + z[:, lat + LANES:lat + 2 * LANES] * sin).T
    low = lax.broadcasted_iota(jnp.int32, kr_t.shape, 0) < MLA_ROPE
    krt_ref[0:LANES, :] = jnp.where(low, kr_t, 0.0).astype(BF16)
    krt_ref[LANES:2 * LANES, :] = jnp.where(low, 0.0, kr_t).astype(BF16)
    cq = _rms(z[:, :MLA_Q_RANK], gq_ref[...]).astype(BF16)
    ckv = _rms(z[:, MLA_Q_RANK:lat], gkv_ref[...])
    q = _dot(cq, wuq_ref[...]) * scale
    qn_ref[...] = q[:, :hn].astype(BF16)
    reps = hr // LANES
    cos_r = jnp.concatenate([cos] * reps, axis=-1)
    sin_r = jnp.concatenate([sin] * reps, axis=-1)
    qr_ref[...] = (q[:, hn:hn + hr] * cos_r + q[:, hn + hr:hn + 2 * hr] * sin_r).astype(BF16)
    knt_ref[...] = _dot(wknt_ref[...], ckv.T.astype(BF16)).astype(BF16)
    v_ref[...] = _dot(ckv.astype(BF16), wv_ref[...]).astype(BF16)


def _mla_attn_kernel(qn_ref, qr_ref, knt_ref, krt_ref, v_ref, o_ref):
    s_len = qn_ref.shape[0]
    tq = TQ_MLA
    tk = TK_MLA
    rows = lax.broadcasted_iota(jnp.int32, (tq, tk), 0)
    cols = lax.broadcasted_iota(jnp.int32, (tq, tk), 1)

    def scores(q0):
        q1 = q0 + tq
        d0 = q0 // tk * tk
        q = jnp.concatenate([qn_ref[q0:q1, :], qr_ref[q0:q1, :]], axis=-1)
        k_diag = jnp.concatenate([knt_ref[:, d0:d0 + tk], krt_ref[:, d0:d0 + tk]], axis=0)
        s_diag = jnp.where(rows + (q0 - d0) >= cols, _dot(q, k_diag), -jnp.inf)
        if not d0:
            return s_diag, None
        k_past = jnp.concatenate([knt_ref[:, 0:d0], krt_ref[:, 0:d0]], axis=0)
        return s_diag, _dot(q, k_past)

    def softmax(q0, s_diag, s_past):
        m = jnp.max(s_diag, axis=-1, keepdims=True)
        if s_past is not None:
            m = jnp.maximum(m, jnp.max(s_past, axis=-1, keepdims=True))
        p_diag = jnp.exp2(s_diag - m)
        l = jnp.sum(p_diag, axis=-1, keepdims=True)
        p_past = None
        if s_past is not None:
            p_past = jnp.exp2(s_past - m)
            l = l + jnp.sum(p_past, axis=-1, keepdims=True)
            p_past = p_past.astype(BF16)
        return q0, p_diag.astype(BF16), p_past, l

    def values(q0, p_diag, p_past, l):
        d0 = q0 // tk * tk
        acc = _dot(p_diag, v_ref[d0:d0 + tk, :])
        if p_past is not None:
            acc = acc + _dot(p_past, v_ref[0:d0, :])
        o_ref[q0:q0 + tq, :] = (acc / l).astype(BF16)

    starts = list(range(0, s_len, tq))
    nxt = scores(starts[0])
    pending = None
    for idx, q0 in enumerate(starts):
        cur = nxt
        if idx + 1 < len(starts):
            nxt = scores(starts[idx + 1])
        if pending is not None:
            values(*pending)
        pending = softmax(q0, *cur)
    values(*pending)


def _mla_mixer(x, g, w_in_x, gq, gkv, w_uq_x, w_kn_t, w_v, cos_t, sin_t):
    b, s, d = x.shape
    hn = MLA_HEADS * MLA_NOPE
    hr = MLA_HEADS * MLA_ROPE
    tm = TM_MLA_PROJ
    row = lambda w: pl.BlockSpec((None, tm, w), lambda bi, i: (bi, i, 0))
    col = lambda r: pl.BlockSpec((None, r, tm), lambda bi, i: (bi, 0, i))
    vec = lambda w: pl.BlockSpec((1, w), lambda bi, i: (0, 0))
    qn, qr, knt, krt, v = pl.pallas_call(
        _mla_proj_kernel,
        grid=(b, s // tm),
        in_specs=[row(d), vec(d), _resident(w_in_x.shape), vec(MLA_Q_RANK), vec(MLA_KV_RANK),
                  _resident(w_uq_x.shape), _resident(w_kn_t.shape), _resident(w_v.shape),
                  row(LANES), row(LANES)],
        out_specs=[row(hn), row(hr), col(hn), col(2 * LANES), row(hn)],
        out_shape=[jax.ShapeDtypeStruct((b, s, hn), BF16), jax.ShapeDtypeStruct((b, s, hr), BF16),
                   jax.ShapeDtypeStruct((b, hn, s), BF16), jax.ShapeDtypeStruct((b, 2 * LANES, s), BF16),
                   jax.ShapeDtypeStruct((b, s, hn), BF16)],
        compiler_params=_params("parallel", "parallel"),
        name="mla_proj",
    )(x, g[None, :], w_in_x, gq[None, :], gkv[None, :], w_uq_x, w_kn_t, w_v,
      cos_t.reshape(b, s, LANES), sin_t.reshape(b, s, LANES))

    head = pl.BlockSpec((None, s, LANES), lambda bi, hd: (bi, 0, hd))
    return pl.pallas_call(
        _mla_attn_kernel,
        grid=(b, MLA_HEADS),
        in_specs=[head,
                  pl.BlockSpec((None, s, LANES), lambda bi, hd: (bi, 0, hd // 2)),
                  pl.BlockSpec((None, LANES, s), lambda bi, hd: (bi, hd, 0)),
                  pl.BlockSpec((None, LANES, s), lambda bi, hd: (bi, hd % 2, 0)),
                  head],
        out_specs=head,
        out_shape=jax.ShapeDtypeStruct((b, s, hn), BF16),
        compiler_params=_params("parallel", "parallel"),
        name="mla_attn",
    )(qn, qr, knt, krt, v)


def _zero_at_sequence_start(carry_ref):
    @pl.when(pl.program_id(1) == 0)
    def _():
        carry_ref[...] = jnp.zeros(carry_ref.shape, F32)


def _causal_conv(z, w_ref, col, carry_ref):
    tm, c = z.shape
    k = w_ref.shape[0]
    head = jnp.concatenate([carry_ref[...], z[0:SUBLANES, :]], axis=0)
    y = w_ref[k - 1:k, col:col + c] * z
    y_head = y[0:SUBLANES, :]
    for j in range(1, k):
        tap = w_ref[k - 1 - j:k - j, col:col + c]
        y = y + tap * pltpu.roll(z, j, 0)
        y_head = y_head + tap * pltpu.roll(head, j, 0)[SUBLANES:, :]
    carry_ref[...] = z[tm - SUBLANES:, :]
    return jnp.concatenate([y_head, y[SUBLANES:, :]], axis=0)


def _sc_kernel(x_ref, g_ref, win_ref, cw_ref, out_ref, buf_ref):
    w = out_ref.shape[-1]
    _zero_at_sequence_start(buf_ref)
    h = _rms(x_ref[...], g_ref[...]).astype(BF16)
    cb = SC_COLS

    def projections(c0):
        return [_dot(h, win_ref[:, part * w + c0:part * w + c0 + cb].astype(BF16)) for part in range(3)]

    nxt = projections(0)
    for c0 in range(0, w, cb):
        zb, zc, zu = nxt
        if c0 + cb < w:
            nxt = projections(c0 + cb)
        y = _causal_conv(zc * zu, cw_ref, c0, buf_ref.at[:, c0:c0 + cb])
        out_ref[:, c0:c0 + cb] = (zb * y).astype(BF16)


def _sc_mixer(x, g, w_in, layer, conv_w):
    b, s, d = x.shape
    w = conv_w.shape[1]
    tm = TM_SC
    row = lambda c: pl.BlockSpec((None, tm, c), lambda bi, i: (bi, i, 0))
    return pl.pallas_call(
        _sc_kernel,
        grid=(b, s // tm),
        in_specs=[row(d), pl.BlockSpec((1, d), lambda bi, i: (0, 0)),
                  _layer_resident(w_in.shape[1:], layer),
                  pl.BlockSpec(conv_w.shape, lambda bi, i: (0, 0))],
        out_specs=row(w),
        out_shape=jax.ShapeDtypeStruct((b, s, w), BF16),
        scratch_shapes=[pltpu.VMEM((SUBLANES, w), F32)],
        compiler_params=_params("parallel", "arbitrary"),
        name="short_conv",
    )(x, g[None, :], w_in, conv_w)


def _gdn_proj_kernel(x_ref, g_ref, win_ref, cw_ref, alog_ref, dtb_ref,
                     q_ref, k_ref, v_ref, gate_ref, bg_ref, buf_ref):
    hk = GDN_HEADS * GDN_DK
    _zero_at_sequence_start(buf_ref)
    h = _rms(x_ref[...], g_ref[...]).astype(BF16)
    cb = GDN_PROJ_COLS
    proj = lambda c0: _dot(h, win_ref[:, c0:c0 + cb])
    gate_cols = list(range(0, hk, cb))
    z_next = proj(0)
    for c0 in range(0, 3 * hk, cb):
        z = z_next
        if c0 + cb < 3 * hk:
            z_next = proj(c0 + cb)
        if gate_cols:
            g0 = gate_cols.pop(0)
            gate_ref[:, g0:g0 + cb] = proj(3 * hk + g0).astype(BF16)
        part, off = divmod(c0, hk)
        ref = (q_ref, k_ref, v_ref)[part]
        y = _silu(_causal_conv(z, cw_ref, c0, buf_ref.at[:, c0:c0 + cb]))
        if part < 2:
            post = GDN_DK ** -0.5 if part == 0 else 1.0
            for d0 in range(0, cb, GDN_DK):
                yh = y[:, d0:d0 + GDN_DK]
                inv = lax.rsqrt(jnp.sum(yh * yh, axis=-1, keepdims=True) + EPS) * post
                ref[:, off + d0:off + d0 + GDN_DK] = (yh * inv).astype(BF16)
        else:
            ref[:, off:off + cb] = y.astype(BF16)
    for g0 in gate_cols:
        gate_ref[:, g0:g0 + cb] = proj(3 * hk + g0).astype(BF16)
    zb = _dot(h, win_ref[:, 4 * hk:4 * hk + LANES])
    beta = jax.nn.sigmoid(zb)
    t = zb + dtb_ref[...]
    softplus = jnp.maximum(t, 0.0) + jnp.log1p(jnp.exp(-jnp.abs(t)))
    decay = -jnp.exp(alog_ref[...]) * softplus
    lane = lax.broadcasted_iota(jnp.int32, zb.shape, 1)
    bg_ref[...] = jnp.where(lane < GDN_HEADS, beta, decay)


def _unit_lower_inverses(ms):
    c = ms[0].shape[0]
    eye = (lax.broadcasted_iota(jnp.int32, (c, c), 0) == lax.broadcasted_iota(jnp.int32, (c, c), 1)).astype(F32)
    invs = [eye - m for m in ms]
    powers = [m.astype(BF16) for m in ms]
    width = 2
    while width < c:
        powers = [_dot(p, p).astype(BF16) for p in powers]
        invs = [inv + _dot(inv.astype(BF16), p) for inv, p in zip(invs, powers)]
        width *= 2
    return invs


def _gdn_chunk_kernel(q_ref, k_ref, v_ref, gate_ref, bg_ref, onorm_ref, out_ref, state_ref):
    c = GDN_CHUNK
    nb, t = q_ref.shape[0], q_ref.shape[1]
    nh = GDN_HEADS

    @pl.when(pl.program_id(1) == 0)
    def _():
        state_ref[...] = jnp.zeros(state_ref.shape, F32)

    rows = lax.broadcasted_iota(jnp.int32, (c, c), 0)
    cols = lax.broadcasted_iota(jnp.int32, (c, c), 1)
    tri = rows >= cols
    strict = rows > cols
    hi = functools.partial(jnp.dot, precision=lax.Precision.HIGHEST, preferred_element_type=F32)
    onorm = onorm_ref[...]
    sl = [slice(hd * GDN_DK, (hd + 1) * GDN_DK) for hd in range(nh)]

    def intra(chunks):
        items = [(bi, r0, hd) for bi, r0 in chunks for hd in range(nh)]
        bgs = {ch: bg_ref[ch[0], ch[1]:ch[1] + c, :] for ch in chunks}
        gcums = {ch: hi(tri.astype(F32), bgs[ch]) for ch in chunks}
        gcums_t = {ch: gcums[ch].T for ch in chunks}
        q = [q_ref[bi, r0:r0 + c, sl[hd]].astype(F32) for bi, r0, hd in items]
        kbf = [k_ref[bi, r0:r0 + c, sl[hd]] for bi, r0, hd in items]
        k = [x.astype(F32) for x in kbf]
        v = [v_ref[bi, r0:r0 + c, sl[hd]].astype(F32) for bi, r0, hd in items]
        beta = [bgs[bi, r0][:, hd:hd + 1] for bi, r0, hd in items]
        gcol = [gcums[bi, r0][:, nh + hd:nh + hd + 1] for bi, r0, hd in items]
        grow = [gcums_t[bi, r0][nh + hd:nh + hd + 1, :] for bi, r0, hd in items]
        decay = [jnp.where(tri, jnp.exp(jnp.where(tri, gc - gr, 0.0)), 0.0) for gc, gr in zip(gcol, grow)]
        kb = [x * b for x, b in zip(k, beta)]
        kk = [lax.dot_general(a.astype(BF16), b, NT_DIMS, preferred_element_type=F32) for a, b in zip(kb, kbf)]
        attn = [lax.dot_general(a.astype(BF16), b, NT_DIMS, preferred_element_type=F32) * d
                for a, b, d in zip(q, kbf, decay)]
        t_inv = [x.astype(BF16) for x in
                 _unit_lower_inverses([jnp.where(strict, a * d, 0.0) for a, d in zip(kk, decay)])]
        eg = [jnp.exp(gc) for gc in gcol]
        u = [_dot(ti, (x * b).astype(BF16)) for ti, x, b in zip(t_inv, v, beta)]
        w = [_dot(ti, (x * e).astype(BF16)).astype(BF16) for ti, x, e in zip(t_inv, kb, eg)]
        qe = [(x * e).astype(BF16) for x, e in zip(q, eg)]
        g_last = [gc[c - 1:c, :] for gc in gcol]
        kd = [(x * jnp.exp(gl - gc)).astype(BF16) for x, gl, gc in zip(k, g_last, gcol)]
        e_last = [jnp.exp(gl) for gl in g_last]
        fields = (u, w, [a.astype(BF16) for a in attn], qe, kd, e_last)
        return {ch: tuple(f[i * nh:(i + 1) * nh] for f in fields) for i, ch in enumerate(chunks)}

    def recur(r0, pres, state):
        u, w, attn, qe, kd, e_last = (sum((list(p[f]) for p in pres), []) for f in range(6))
        sb = [x.astype(BF16) for x in state]
        vnb = [(a - _dot(b, s)).astype(BF16) for a, b, s in zip(u, w, sb)]
        o = [_dot(x, s) + _dot(a, vn) for x, s, a, vn in zip(qe, sb, attn, vnb)]
        state = [s * e + lax.dot_general(x, vn, TN_DIMS, preferred_element_type=F32)
                 for s, e, x, vn in zip(state, e_last, kd, vnb)]
        for bi in range(nb):
            gated = [(_rms(o[bi * nh + hd], onorm)
                      * _silu(gate_ref[bi, r0:r0 + c, sl[hd]].astype(F32))).astype(BF16) for hd in range(nh)]
            out_ref[bi, r0:r0 + c, :] = jnp.concatenate(gated, axis=-1)
        return state

    starts = list(range(0, t, c))
    pre = intra([(bi, r0) for r0 in starts for bi in range(nb)])
    state = [state_ref[i] for i in range(nb * nh)]
    for r0 in starts:
        state = recur(r0, [pre[bi, r0] for bi in range(nb)], state)
    for i in range(nb * nh):
        state_ref[i] = state[i]


def _gdn_mixer(x, g, w_in_x, conv_w, a_log, dt_bias, o_norm):
    b, s, d = x.shape
    hk = GDN_HEADS * GDN_DK
    lanes_pad = jnp.zeros((LANES - 2 * GDN_HEADS,), F32)
    zeros_h = jnp.zeros((GDN_HEADS,), F32)
    alog_row = jnp.concatenate([zeros_h, a_log, lanes_pad])[None, :]
    dtb_row = jnp.concatenate([zeros_h, dt_bias, lanes_pad])[None, :]
    tm = TM_GDN_PROJ
    row = lambda tt, c: pl.BlockSpec((None, tt, c), lambda bi, i: (bi, i, 0))
    vec = lambda c: pl.BlockSpec((1, c), lambda bi, i: (0, 0))
    q, k, v, gate, bg = pl.pallas_call(
        _gdn_proj_kernel,
        grid=(b, s // tm),
        in_specs=[row(tm, d), vec(d), _resident(w_in_x.shape),
                  pl.BlockSpec(conv_w.shape, lambda bi, i: (0, 0)), vec(LANES), vec(LANES)],
        out_specs=[row(tm, hk), row(tm, hk), row(tm, hk), row(tm, hk), row(tm, LANES)],
        out_shape=[jax.ShapeDtypeStruct((b, s, hk), BF16)] * 4 + [jax.ShapeDtypeStruct((b, s, LANES), F32)],
        scratch_shapes=[pltpu.VMEM((SUBLANES, 3 * hk), F32)],
        compiler_params=_params("parallel", "arbitrary"),
        name="gdn_proj",
    )(x, g[None, :], w_in_x, conv_w, alog_row, dtb_row)

    tt = T_GDN
    nb = GDN_SEQS
    seqs = lambda c: pl.BlockSpec((nb, tt, c), lambda bi, i: (bi, i, 0))
    return pl.pallas_call(
        _gdn_chunk_kernel,
        grid=(b // nb, s // tt),
        in_specs=[seqs(hk), seqs(hk), seqs(hk), seqs(hk), seqs(LANES), vec(GDN_DV)],
        out_specs=seqs(hk),
        out_shape=jax.ShapeDtypeStruct((b, s, hk), BF16),
        scratch_shapes=[pltpu.VMEM((nb * GDN_HEADS, GDN_DK, GDN_DV), F32)],
        compiler_params=_params("parallel", "arbitrary"),
        name="gdn_chunk",
    )(q, k, v, gate, bg, o_norm[None, :])


def kernel(x, mem, positions, mla_w_in, mla_q_norm, mla_kv_norm, mla_w_uq, mla_w_ukv, mla_w_o, gdn_w_in, gdn_conv_w, gdn_a_log, gdn_dt_bias, gdn_o_norm, gdn_w_o, sc_w_in, sc_conv_w, sc_w_o, norm_mix, norm_mem, norm_mlp, xa_w_q, xa_w_kv, xa_w_o, mlp_w1, mlp_w2, mem_norm, final_norm):
    b, s, d = x.shape
    depth = norm_mix.shape[0]
    m = mem.shape[1]
    cos_t, sin_t = _rope_tables(positions)
    kv_all = _mem_kv(mem.reshape(b * m, d), mem_norm, xa_w_kv).reshape(depth, b, m, 2 * d)
    for i in range(depth):
        j = i // N_MIXERS
        kind = i % N_MIXERS
        if kind == 0:
            w_in_x, w_uq_x, w_kn_t, w_v = _mla_weights(mla_w_in[j], mla_w_uq[j], mla_w_ukv[j])
            pre = _mla_mixer(x, norm_mix[i], w_in_x, mla_q_norm[j], mla_kv_norm[j], w_uq_x, w_kn_t, w_v,
                             cos_t, sin_t)
            w_mix = mla_w_o
        elif kind == 1:
            pad = jnp.zeros((d, LANES - 2 * GDN_HEADS), F32)
            w_in_x = jnp.concatenate([gdn_w_in[j], pad], axis=1).astype(BF16)
            pre = _gdn_mixer(x, norm_mix[i], w_in_x, gdn_conv_w[j], gdn_a_log[j], gdn_dt_bias[j],
                             gdn_o_norm[j])
            w_mix = gdn_w_o
        else:
            pre = _sc_mixer(x, norm_mix[i], sc_w_in, j, sc_conv_w[j])
            w_mix = sc_w_o
        x = _tail(x, pre, w_mix, j, norm_mem[i], xa_w_q, kv_all, i, xa_w_o)
        x = _mlp(x.reshape(b * s, d), norm_mlp[i], mlp_w1, mlp_w2, i,
                 final_norm, i == depth - 1).reshape(b, s, d)
    return x
```

```python
import functools

import jax
import jax.numpy as jnp
from jax import lax
from jax.experimental import pallas as pl
from jax.experimental.pallas import tpu as pltpu

F32 = jnp.float32
BF16 = jnp.bfloat16

EPS = 1e-6
ROPE_THETA = 10000.0
LOG2E = 1.4426950408889634
N_MIXERS = 3

MLA_HEADS = 8
MLA_NOPE = 128
MLA_ROPE = 64
MLA_V = 128
MLA_Q_RANK = 384
MLA_KV_RANK = 256

GDN_HEADS = 8
GDN_DK = 128
GDN_DV = 128
GDN_CONV = 4
GDN_CHUNK = 64

SC_CONV = 3
X_HEADS = 4

LANES = 128
SUBLANES = 8
VMEM_LIMIT_BYTES = 56 * 1024 * 1024

TM_TABLE = 1024
TM_MEM = 1024
TM_TAIL = 1024
TM_MLP = 512
TM_MLA_PROJ = 512
TQ_MLA = 256
TK_MLA = 256
TM_SC = 512
TM_GDN_PROJ = 256
T_GDN = 128
GDN_SEQS = 2
GDN_PROJ_COLS = 256
SC_COLS = 256
FF_CHUNK = 1024

NT_DIMS = (((1,), (1,)), ((), ()))
TN_DIMS = (((0,), (0,)), ((), ()))


def _tile(extent, tile):
    assert extent % tile == 0, (extent, tile)
    return tile


def _params(*sem):
    return pltpu.CompilerParams(dimension_semantics=sem, vmem_limit_bytes=VMEM_LIMIT_BYTES)


def _resident(shape):
    nd = len(shape)
    return pl.BlockSpec(shape, lambda *_: (0,) * nd, pipeline_mode=pl.Buffered(1))


def _layer_resident(shape, layer):
    nd = len(shape)
    return pl.BlockSpec((None,) + shape, lambda *_: (layer,) + (0,) * nd, pipeline_mode=pl.Buffered(1))


def _rms(x, g):
    return x * lax.rsqrt(jnp.mean(x * x, axis=-1, keepdims=True) + EPS) * g


def _dot(a, b):
    return jnp.dot(a, b, preferred_element_type=F32)


def _sigmoid(x):
    return 1.0 / (1.0 + jnp.exp2(x * -LOG2E))


def _silu(x):
    return x * _sigmoid(x)


def _rope_table_kernel(pos_ref, freq_ref, cos_ref, sin_ref):
    ang = pos_ref[...] * freq_ref[...]
    cos_ref[...] = jnp.cos(ang)
    sin_ref[...] = jnp.sin(ang)


def _rope_tables(positions):
    n = positions.size
    pos = jnp.broadcast_to(positions.reshape(n, 1).astype(F32), (n, LANES))
    inv_freq = ROPE_THETA ** (-jnp.arange(0, MLA_ROPE, 2, dtype=F32) / MLA_ROPE)
    freq = jnp.tile(inv_freq, 2 * LANES // MLA_ROPE)[None, :]
    tm = _tile(n, TM_TABLE)
    row = pl.BlockSpec((tm, LANES), lambda i: (i, 0))
    return pl.pallas_call(
        _rope_table_kernel,
        grid=(n // tm,),
        in_specs=[row, pl.BlockSpec((1, LANES), lambda i: (0, 0))],
        out_specs=[row, row],
        out_shape=[jax.ShapeDtypeStruct((n, LANES), F32)] * 2,
        compiler_params=_params("parallel"),
        name="rope_tables",
    )(pos, freq)


def _mem_kv_kernel(mem_ref, g_ref, w_ref, out_ref):
    mn = _rms(mem_ref[...], g_ref[...]).astype(BF16)
    out_ref[0] = _dot(mn, w_ref[0].astype(BF16)).astype(BF16)


def _mem_kv(mem2d, mem_norm, w_kv):
    n, d = mem2d.shape
    depth, _, d2 = w_kv.shape
    tm = _tile(n, min(TM_MEM, n))
    return pl.pallas_call(
        _mem_kv_kernel,
        grid=(depth, n // tm),
        in_specs=[
            pl.BlockSpec((tm, d), lambda l, i: (i, 0)),
            pl.BlockSpec((1, d), lambda l, i: (0, 0)),
            pl.BlockSpec((1, d, d2), lambda l, i: (l, 0, 0)),
        ],
        out_specs=pl.BlockSpec((1, tm, d2), lambda l, i: (l, i, 0)),
        out_shape=jax.ShapeDtypeStruct((depth, n, d2), BF16),
        compiler_params=_params("parallel", "parallel"),
        name="mem_kv",
    )(mem2d, mem_norm[None, :], w_kv)


def _tail_kernel(x_ref, pre_ref, wmix_ref, g_ref, wq_ref, kv_ref, wo_ref, out_ref):
    d = x_ref.shape[-1]
    dh = d // X_HEADS
    x = x_ref[...] + _dot(pre_ref[...], wmix_ref[...].astype(BF16))
    h = _rms(x, g_ref[...]).astype(BF16)
    q = (_dot(h, wq_ref[...].astype(BF16)) * (dh ** -0.5 * LOG2E)).astype(BF16)
    scores = [lax.dot_general(q[:, hd * dh:(hd + 1) * dh], kv_ref[:, hd * dh:(hd + 1) * dh], NT_DIMS,
                              preferred_element_type=F32) for hd in range(X_HEADS)]
    heads = []
    for hd, s in enumerate(scores):
        p = jnp.exp2(s - jnp.max(s, axis=-1, keepdims=True))
        l = jnp.sum(p, axis=-1, keepdims=True)
        heads.append((_dot(p.astype(BF16), kv_ref[:, d + hd * dh:d + (hd + 1) * dh]) / l).astype(BF16))
    o = jnp.concatenate(heads, axis=-1)
    out_ref[...] = x + _dot(o, wo_ref[...].astype(BF16))


def _tail(x, pre, w_mix, mix_layer, g_mem, w_q, kv_all, layer, w_o):
    b, s, d = x.shape
    m = kv_all.shape[2]
    tm = _tile(s, TM_TAIL)
    row = pl.BlockSpec((None, tm, d), lambda bi, i: (bi, i, 0))
    return pl.pallas_call(
        _tail_kernel,
        grid=(b, s // tm),
        in_specs=[
            row, row, _layer_resident((d, d), mix_layer),
            pl.BlockSpec((1, d), lambda bi, i: (0, 0)),
            _layer_resident((d, d), layer),
            pl.BlockSpec((None, None, m, 2 * d), lambda bi, i: (layer, bi, 0, 0)),
            _layer_resident((d, d), layer),
        ],
        out_specs=row,
        out_shape=jax.ShapeDtypeStruct((b, s, d), F32),
        compiler_params=_params("parallel", "parallel"),
        name="tail",
    )(x, pre, w_mix, g_mem[None, :], w_q, kv_all, w_o)


def _mlp_kernel(x_ref, g_ref, w1_ref, w2_ref, gf_ref, out_ref, *, final):
    x = x_ref[...]
    h = _rms(x, g_ref[...]).astype(BF16)
    acc = x
    up = lambda c: _dot(h, w1_ref[:, c:c + FF_CHUNK].astype(BF16))
    ff = w1_ref.shape[1]
    a_next = up(0)
    for c in range(0, ff, FF_CHUNK):
        a = jnp.maximum(a_next, 0.0)
        if c + FF_CHUNK < ff:
            a_next = up(c + FF_CHUNK)
        acc = acc + _dot((a * a).astype(BF16), w2_ref[c:c + FF_CHUNK, :].astype(BF16))
    if final:
        acc = _rms(acc, gf_ref[...])
    out_ref[...] = acc


def _mlp(x2d, g, w1, w2, layer, g_final, final):
    n, d = x2d.shape
    ff = w1.shape[2]
    tm = _tile(n, TM_MLP)
    row = pl.BlockSpec((tm, d), lambda i: (i, 0))
    vec = pl.BlockSpec((1, d), lambda i: (0, 0))
    return pl.pallas_call(
        functools.partial(_mlp_kernel, final=final),
        grid=(n // tm,),
        in_specs=[row, vec, _layer_resident((d, ff), layer), _layer_resident((ff, d), layer), vec],
        out_specs=row,
        out_shape=jax.ShapeDtypeStruct((n, d), F32),
        compiler_params=_params("parallel"),
        name="mlp",
    )(x2d, g[None, :], w1, w2, g_final[None, :])


def _mla_weights(w_in, w_uq, w_ukv):
    half = MLA_ROPE // 2

    def swap_cols(w):
        return jnp.concatenate([-w[..., half:], w[..., :half]], axis=-1)

    lat = MLA_Q_RANK + MLA_KV_RANK
    kr = w_in[:, lat:]
    w_in_x = jnp.concatenate([w_in[:, :lat], kr, kr, swap_cols(kr), swap_cols(kr)], axis=1)
    uq = w_uq.reshape(MLA_Q_RANK, MLA_HEADS, MLA_NOPE + MLA_ROPE)
    qn = uq[:, :, :MLA_NOPE].reshape(MLA_Q_RANK, -1)
    qr = uq[:, :, MLA_NOPE:]
    w_uq_x = jnp.concatenate(
        [qn, qr.reshape(MLA_Q_RANK, -1), swap_cols(qr).reshape(MLA_Q_RANK, -1)], axis=1)
    ukv = w_ukv.reshape(MLA_KV_RANK, MLA_HEADS, MLA_NOPE + MLA_V)
    w_kn_t = ukv[:, :, :MLA_NOPE].reshape(MLA_KV_RANK, -1).T
    w_v = ukv[:, :, MLA_NOPE:].reshape(MLA_KV_RANK, -1)
    return w_in_x.astype(BF16), w_uq_x.astype(BF16), w_kn_t.astype(BF16), w_v.astype(BF16)


def _mla_proj_kernel(x_ref, g_ref, win_ref, gq_ref, gkv_ref, wuq_ref, wknt_ref, wv_ref, cos_ref, sin_ref,
                     qn_ref, qr_ref, knt_ref, krt_ref, v_ref):
    hn = MLA_HEADS * MLA_NOPE
    hr = MLA_HEADS * MLA_ROPE
    lat = MLA_Q_RANK + MLA_KV_RANK
    scale = (MLA_NOPE + MLA_ROPE) ** -0.5 * LOG2E
    h = _rms(x_ref[...], g_ref[...]).astype(BF16)
    z = _dot(h, win_ref[...])
    cos = cos_ref[...]
    sin = sin_ref[...]
    kr_t = (z[:, lat:lat + LANES] * cos + z[:, lat + LANES:lat + 2 * LANES] * sin).T
    low = lax.broadcasted_iota(jnp.int32, kr_t.shape, 0) < MLA_ROPE
    krt_ref[0:LANES, :] = jnp.where(low, kr_t, 0.0).astype(BF16)
    krt_ref[LANES:2 * LANES, :] = jnp.where(low, 0.0, kr_t).astype(BF16)
    cq = _rms(z[:, :MLA_Q_RANK], gq_ref[...]).astype(BF16)
    ckv = _rms(z[:, MLA_Q_RANK:lat], gkv_ref[...])
    q = _dot(cq, wuq_ref[...]) * scale
    qn_ref[...] = q[:, :hn].astype(BF16)
    reps = hr // LANES
    cos_r = jnp.concatenate([cos] * reps, axis=-1)
    sin_r = jnp.concatenate([sin] * reps, axis=-1)
    qr_ref[...] = (q[:, hn:hn + hr] * cos_r + q[:, hn + hr:hn + 2 * hr] * sin_r).astype(BF16)
    knt_ref[...] = _dot(wknt_ref[...], ckv.T.astype(BF16)).astype(BF16)
    v_ref[...] = _dot(ckv.astype(BF16), wv_ref[...]).astype(BF16)


def _mla_attn_kernel(qn_ref, qr_ref, knt_ref, krt_ref, v_ref, o_ref):
    s_len = qn_ref.shape[0]
    tq = TQ_MLA
    tk = TK_MLA
    rows = lax.broadcasted_iota(jnp.int32, (tq, tk), 0)
    cols = lax.broadcasted_iota(jnp.int32, (tq, tk), 1)
    lanes = lambda hl: slice(hl * LANES, (hl + 1) * LANES)

    def scores(q0, hl):
        q1 = q0 + tq
        d0 = q0 // tk * tk
        q = jnp.concatenate([qn_ref[q0:q1, lanes(hl)], qr_ref[q0:q1, :]], axis=-1)
        k_diag = jnp.concatenate([knt_ref[lanes(hl), d0:d0 + tk], krt_ref[lanes(hl), d0:d0 + tk]], axis=0)
        s_diag = jnp.where(rows + (q0 - d0) >= cols, _dot(q, k_diag), -jnp.inf)
        if not d0:
            return s_diag, None
        k_past = jnp.concatenate([knt_ref[lanes(hl), 0:d0], krt_ref[lanes(hl), 0:d0]], axis=0)
        return s_diag, _dot(q, k_past)

    def softmax(unit, s_diag, s_past):
        m = jnp.max(s_diag, axis=-1, keepdims=True)
        if s_past is not None:
            m = jnp.maximum(m, jnp.max(s_past, axis=-1, keepdims=True))
        p_diag = jnp.exp2(s_diag - m)
        l = jnp.sum(p_diag, axis=-1, keepdims=True)
        p_past = None
        if s_past is not None:
            p_past = jnp.exp2(s_past - m)
            l = l + jnp.sum(p_past, axis=-1, keepdims=True)
            p_past = p_past.astype(BF16)
        return unit, p_diag.astype(BF16), p_past, l

    def values(unit, p_diag, p_past, l):
        q0, hl = unit
        d0 = q0 // tk * tk
        acc = _dot(p_diag, v_ref[d0:d0 + tk, lanes(hl)])
        if p_past is not None:
            acc = acc + _dot(p_past, v_ref[0:d0, lanes(hl)])
        o_ref[q0:q0 + tq, lanes(hl)] = (acc / l).astype(BF16)

    units = [(q0, hl) for q0 in range(0, s_len, tq) for hl in range(2)]
    nxt = scores(*units[0])
    pending = None
    for idx, unit in enumerate(units):
        cur = nxt
        if idx + 1 < len(units):
            nxt = scores(*units[idx + 1])
        if pending is not None:
            values(*pending)
        pending = softmax(unit, *cur)
    values(*pending)


def _mla_mixer(x, g, w_in_x, gq, gkv, w_uq_x, w_kn_t, w_v, cos_t, sin_t):
    b, s, d = x.shape
    hn = MLA_HEADS * MLA_NOPE
    hr = MLA_HEADS * MLA_ROPE
    tm = _tile(s, TM_MLA_PROJ)
    _tile(s, TK_MLA)
    row = lambda w: pl.BlockSpec((None, tm, w), lambda bi, i: (bi, i, 0))
    col = lambda r: pl.BlockSpec((None, r, tm), lambda bi, i: (bi, 0, i))
    vec = lambda w: pl.BlockSpec((1, w), lambda bi, i: (0, 0))
    qn, qr, knt, krt, v = pl.pallas_call(
        _mla_proj_kernel,
        grid=(b, s // tm),
        in_specs=[row(d), vec(d), _resident(w_in_x.shape), vec(MLA_Q_RANK), vec(MLA_KV_RANK),
                  _resident(w_uq_x.shape), _resident(w_kn_t.shape), _resident(w_v.shape),
                  row(LANES), row(LANES)],
        out_specs=[row(hn), row(hr), col(hn), col(2 * LANES), row(hn)],
        out_shape=[jax.ShapeDtypeStruct((b, s, hn), BF16), jax.ShapeDtypeStruct((b, s, hr), BF16),
                   jax.ShapeDtypeStruct((b, hn, s), BF16), jax.ShapeDtypeStruct((b, 2 * LANES, s), BF16),
                   jax.ShapeDtypeStruct((b, s, hn), BF16)],
        compiler_params=_params("parallel", "parallel"),
        name="mla_proj",
    )(x, g[None, :], w_in_x, gq[None, :], gkv[None, :], w_uq_x, w_kn_t, w_v,
      cos_t.reshape(b, s, LANES), sin_t.reshape(b, s, LANES))

    pair = pl.BlockSpec((None, s, 2 * LANES), lambda bi, hp: (bi, 0, hp))
    return pl.pallas_call(
        _mla_attn_kernel,
        grid=(b, MLA_HEADS // 2),
        in_specs=[pair,
                  pl.BlockSpec((None, s, LANES), lambda bi, hp: (bi, 0, hp)),
                  pl.BlockSpec((None, 2 * LANES, s), lambda bi, hp: (bi, hp, 0)),
                  pl.BlockSpec((None, 2 * LANES, s), lambda bi, hp: (bi, 0, 0)),
                  pair],
        out_specs=pair,
        out_shape=jax.ShapeDtypeStruct((b, s, hn), BF16),
        compiler_params=_params("parallel", "parallel"),
        name="mla_attn",
    )(qn, qr, knt, krt, v)


def _zero_at_sequence_start(carry_ref):
    @pl.when(pl.program_id(1) == 0)
    def _():
        carry_ref[...] = jnp.zeros(carry_ref.shape, F32)


def _causal_conv(z, w_ref, col, carry_ref):
    tm, c = z.shape
    k = w_ref.shape[0]
    head = jnp.concatenate([carry_ref[...], z[0:SUBLANES, :]], axis=0)
    y = w_ref[k - 1:k, col:col + c] * z
    y_head = y[0:SUBLANES, :]
    for j in range(1, k):
        tap = w_ref[k - 1 - j:k - j, col:col + c]
        y = y + tap * pltpu.roll(z, j, 0)
        y_head = y_head + tap * pltpu.roll(head, j, 0)[SUBLANES:, :]
    carry_ref[...] = z[tm - SUBLANES:, :]
    return jnp.concatenate([y_head, y[SUBLANES:, :]], axis=0)


def _sc_kernel(x_ref, g_ref, win_ref, cw_ref, out_ref, buf_ref):
    w = out_ref.shape[-1]
    _zero_at_sequence_start(buf_ref)
    h = _rms(x_ref[...], g_ref[...]).astype(BF16)
    cb = SC_COLS

    def projections(c0):
        return [_dot(h, win_ref[:, part * w + c0:part * w + c0 + cb].astype(BF16)) for part in range(3)]

    nxt = projections(0)
    for c0 in range(0, w, cb):
        zb, zc, zu = nxt
        if c0 + cb < w:
            nxt = projections(c0 + cb)
        y = _causal_conv(zc * zu, cw_ref, c0, buf_ref.at[:, c0:c0 + cb])
        out_ref[:, c0:c0 + cb] = (zb * y).astype(BF16)


def _sc_mixer(x, g, w_in, layer, conv_w):
    b, s, d = x.shape
    w = conv_w.shape[1]
    tm = _tile(s, TM_SC)
    row = lambda c: pl.BlockSpec((None, tm, c), lambda bi, i: (bi, i, 0))
    return pl.pallas_call(
        _sc_kernel,
        grid=(b, s // tm),
        in_specs=[row(d), pl.BlockSpec((1, d), lambda bi, i: (0, 0)),
                  _layer_resident(w_in.shape[1:], layer),
                  pl.BlockSpec(conv_w.shape, lambda bi, i: (0, 0))],
        out_specs=row(w),
        out_shape=jax.ShapeDtypeStruct((b, s, w), BF16),
        scratch_shapes=[pltpu.VMEM((SUBLANES, w), F32)],
        compiler_params=_params("parallel", "arbitrary"),
        name="short_conv",
    )(x, g[None, :], w_in, conv_w)


def _gdn_proj_kernel(x_ref, g_ref, win_ref, cw_ref, alog_ref, dtb_ref,
                     q_ref, k_ref, v_ref, gate_ref, bg_ref, buf_ref):
    hk = GDN_HEADS * GDN_DK
    _zero_at_sequence_start(buf_ref)
    h = _rms(x_ref[...], g_ref[...]).astype(BF16)
    cb = GDN_PROJ_COLS
    proj = lambda c0: _dot(h, win_ref[:, c0:c0 + cb])
    gate_cols = list(range(0, hk, cb))
    z_next = proj(0)
    for c0 in range(0, 3 * hk, cb):
        z = z_next
        if c0 + cb < 3 * hk:
            z_next = proj(c0 + cb)
        if gate_cols:
            g0 = gate_cols.pop(0)
            gate_ref[:, g0:g0 + cb] = proj(3 * hk + g0).astype(BF16)
        part, off = divmod(c0, hk)
        ref = (q_ref, k_ref, v_ref)[part]
        y = _silu(_causal_conv(z, cw_ref, c0, buf_ref.at[:, c0:c0 + cb]))
        if part < 2:
            post = GDN_DK ** -0.5 if part == 0 else 1.0
            for d0 in range(0, cb, GDN_DK):
                yh = y[:, d0:d0 + GDN_DK]
                inv = lax.rsqrt(jnp.sum(yh * yh, axis=-1, keepdims=True) + EPS) * post
                ref[:, off + d0:off + d0 + GDN_DK] = (yh * inv).astype(BF16)
        else:
            ref[:, off:off + cb] = y.astype(BF16)
    for g0 in gate_cols:
        gate_ref[:, g0:g0 + cb] = proj(3 * hk + g0).astype(BF16)
    zb = _dot(h, win_ref[:, 4 * hk:4 * hk + LANES])
    beta = jax.nn.sigmoid(zb)
    t = zb + dtb_ref[...]
    softplus = jnp.maximum(t, 0.0) + jnp.log1p(jnp.exp(-jnp.abs(t)))
    decay = -jnp.exp(alog_ref[...]) * softplus
    lane = lax.broadcasted_iota(jnp.int32, zb.shape, 1)
    bg_ref[...] = jnp.where(lane < GDN_HEADS, beta, decay)


def _cumsum_rows(x):
    n = x.shape[0]
    row = lax.broadcasted_iota(jnp.int32, x.shape, 0)
    shift = 1
    while shift < n:
        x = x + jnp.where(row >= shift, pltpu.roll(x, shift, 0), 0.0)
        shift *= 2
    return x


def _unit_lower_inverses(ms):
    c = ms[0].shape[0]
    eye = (lax.broadcasted_iota(jnp.int32, (c, c), 0) == lax.broadcasted_iota(jnp.int32, (c, c), 1)).astype(F32)
    invs = [eye - m for m in ms]
    powers = [m.astype(BF16) for m in ms]
    width = 2
    while width < c:
        powers = [_dot(p, p).astype(BF16) for p in powers]
        invs = [inv + _dot(inv.astype(BF16), p) for inv, p in zip(invs, powers)]
        width *= 2
    return invs


def _gdn_chunk_kernel(q_ref, k_ref, v_ref, gate_ref, bg_ref, onorm_ref, out_ref, state_ref):
    c = GDN_CHUNK
    nb, t = q_ref.shape[0], q_ref.shape[1]
    nh = GDN_HEADS

    @pl.when(pl.program_id(1) == 0)
    def _():
        state_ref[...] = jnp.zeros(state_ref.shape, F32)

    rows = lax.broadcasted_iota(jnp.int32, (c, c), 0)
    cols = lax.broadcasted_iota(jnp.int32, (c, c), 1)
    tri = rows >= cols
    strict = rows > cols
    onorm = onorm_ref[...]
    sl = [slice(hd * GDN_DK, (hd + 1) * GDN_DK) for hd in range(nh)]

    def intra(chunks):
        items = [(bi, r0, hd) for bi, r0 in chunks for hd in range(nh)]
        bgs = {ch: bg_ref[ch[0], ch[1]:ch[1] + c, :] for ch in chunks}
        gcums = {ch: _cumsum_rows(bgs[ch]) for ch in chunks}
        gcums_t = {ch: gcums[ch].T for ch in chunks}
        egs = {ch: jnp.exp(gcums[ch]) for ch in chunks}
        eds = {ch: jnp.exp(gcums[ch][c - 1:c, :] - gcums[ch]) for ch in chunks}
        column = lambda a, lane: jnp.broadcast_to(a[:, lane:lane + 1], (c, GDN_DK))
        q = [q_ref[bi, r0:r0 + c, sl[hd]].astype(F32) for bi, r0, hd in items]
        kbf = [k_ref[bi, r0:r0 + c, sl[hd]] for bi, r0, hd in items]
        k = [x.astype(F32) for x in kbf]
        v = [v_ref[bi, r0:r0 + c, sl[hd]].astype(F32) for bi, r0, hd in items]
        beta = [column(bgs[bi, r0], hd) for bi, r0, hd in items]
        eg = [column(egs[bi, r0], nh + hd) for bi, r0, hd in items]
        ed = [column(eds[bi, r0], nh + hd) for bi, r0, hd in items]
        gcol = [gcums[bi, r0][:, nh + hd:nh + hd + 1] for bi, r0, hd in items]
        grow = [gcums_t[bi, r0][nh + hd:nh + hd + 1, :] for bi, r0, hd in items]
        decay = [jnp.where(tri, jnp.exp(jnp.where(tri, gc - gr, 0.0)), 0.0) for gc, gr in zip(gcol, grow)]
        kb = [x * b for x, b in zip(k, beta)]
        kk = [lax.dot_general(a.astype(BF16), b, NT_DIMS, preferred_element_type=F32) for a, b in zip(kb, kbf)]
        attn = [lax.dot_general(a.astype(BF16), b, NT_DIMS, preferred_element_type=F32) * d
                for a, b, d in zip(q, kbf, decay)]
        t_inv = [x.astype(BF16) for x in
                 _unit_lower_inverses([jnp.where(strict, a * d, 0.0) for a, d in zip(kk, decay)])]
        u = [_dot(ti, (x * b).astype(BF16)) for ti, x, b in zip(t_inv, v, beta)]
        w = [_dot(ti, (x * e).astype(BF16)).astype(BF16) for ti, x, e in zip(t_inv, kb, eg)]
        qe = [(x * e).astype(BF16) for x, e in zip(q, eg)]
        kd = [(x * e).astype(BF16) for x, e in zip(k, ed)]
        e_last = [egs[bi, r0][c - 1:c, nh + hd:nh + hd + 1] for bi, r0, hd in items]
        fields = (u, w, [a.astype(BF16) for a in attn], qe, kd, e_last)
        return {ch: tuple(f[i * nh:(i + 1) * nh] for f in fields) for i, ch in enumerate(chunks)}

    def recur(r0, pres, state):
        u, w, attn, qe, kd, e_last = (sum((list(p[f]) for p in pres), []) for f in range(6))
        sb = [x.astype(BF16) for x in state]
        vnb = [(a - _dot(b, s)).astype(BF16) for a, b, s in zip(u, w, sb)]
        o = [_dot(x, s) + _dot(a, vn) for x, s, a, vn in zip(qe, sb, attn, vnb)]
        state = [s * e + lax.dot_general(x, vn, TN_DIMS, preferred_element_type=F32)
                 for s, e, x, vn in zip(state, e_last, kd, vnb)]
        for bi in range(nb):
            gated = [(_rms(o[bi * nh + hd], onorm)
                      * _silu(gate_ref[bi, r0:r0 + c, sl[hd]].astype(F32))).astype(BF16) for hd in range(nh)]
            out_ref[bi, r0:r0 + c, :] = jnp.concatenate(gated, axis=-1)
        return state

    starts = list(range(0, t, c))
    pre = intra([(bi, r0) for r0 in starts for bi in range(nb)])
    state = [state_ref[i] for i in range(nb * nh)]
    for r0 in starts:
        state = recur(r0, [pre[bi, r0] for bi in range(nb)], state)
    for i in range(nb * nh):
        state_ref[i] = state[i]


def _gdn_mixer(x, g, w_in_x, conv_w, a_log, dt_bias, o_norm):
    b, s, d = x.shape
    hk = GDN_HEADS * GDN_DK
    lanes_pad = jnp.zeros((LANES - 2 * GDN_HEADS,), F32)
    zeros_h = jnp.zeros((GDN_HEADS,), F32)
    alog_row = jnp.concatenate([zeros_h, a_log, lanes_pad])[None, :]
    dtb_row = jnp.concatenate([zeros_h, dt_bias, lanes_pad])[None, :]
    tm = _tile(s, TM_GDN_PROJ)
    row = lambda tt, c: pl.BlockSpec((None, tt, c), lambda bi, i: (bi, i, 0))
    vec = lambda c: pl.BlockSpec((1, c), lambda bi, i: (0, 0))
    q, k, v, gate, bg = pl.pallas_call(
        _gdn_proj_kernel,
        grid=(b, s // tm),
        in_specs=[row(tm, d), vec(d), _resident(w_in_x.shape),
                  pl.BlockSpec(conv_w.shape, lambda bi, i: (0, 0)), vec(LANES), vec(LANES)],
        out_specs=[row(tm, hk), row(tm, hk), row(tm, hk), row(tm, hk), row(tm, LANES)],
        out_shape=[jax.ShapeDtypeStruct((b, s, hk), BF16)] * 4 + [jax.ShapeDtypeStruct((b, s, LANES), F32)],
        scratch_shapes=[pltpu.VMEM((SUBLANES, 3 * hk), F32)],
        compiler_params=_params("parallel", "arbitrary"),
        name="gdn_proj",
    )(x, g[None, :], w_in_x, conv_w, alog_row, dtb_row)

    tt = _tile(s, T_GDN)
    nb = _tile(b, GDN_SEQS)
    seqs = lambda c: pl.BlockSpec((nb, tt, c), lambda bi, i: (bi, i, 0))
    return pl.pallas_call(
        _gdn_chunk_kernel,
        grid=(b // nb, s // tt),
        in_specs=[seqs(hk), seqs(hk), seqs(hk), seqs(hk), seqs(LANES), vec(GDN_DV)],
        out_specs=seqs(hk),
        out_shape=jax.ShapeDtypeStruct((b, s, hk), BF16),
        scratch_shapes=[pltpu.VMEM((nb * GDN_HEADS, GDN_DK, GDN_DV), F32)],
        compiler_params=_params("parallel", "arbitrary"),
        name="gdn_chunk",
    )(q, k, v, gate, bg, o_norm[None, :])


def kernel(x, mem, positions, mla_w_in, mla_q_norm, mla_kv_norm, mla_w_uq, mla_w_ukv, mla_w_o, gdn_w_in, gdn_conv_w, gdn_a_log, gdn_dt_bias, gdn_o_norm, gdn_w_o, sc_w_in, sc_conv_w, sc_w_o, norm_mix, norm_mem, norm_mlp, xa_w_q, xa_w_kv, xa_w_o, mlp_w1, mlp_w2, mem_norm, final_norm):
    b, s, d = x.shape
    depth = norm_mix.shape[0]
    m = mem.shape[1]
    cos_t, sin_t = _rope_tables(positions)
    kv_all = _mem_kv(mem.reshape(b * m, d), mem_norm, xa_w_kv).reshape(depth, b, m, 2 * d)
    for i in range(depth):
        j = i // N_MIXERS
        kind = i % N_MIXERS
        if kind == 0:
            w_in_x, w_uq_x, w_kn_t, w_v = _mla_weights(mla_w_in[j], mla_w_uq[j], mla_w_ukv[j])
            pre = _mla_mixer(x, norm_mix[i], w_in_x, mla_q_norm[j], mla_kv_norm[j], w_uq_x, w_kn_t, w_v,
                             cos_t, sin_t)
            w_mix = mla_w_o
        elif kind == 1:
            pad = jnp.zeros((d, LANES - 2 * GDN_HEADS), F32)
            w_in_x = jnp.concatenate([gdn_w_in[j], pad], axis=1).astype(BF16)
            pre = _gdn_mixer(x, norm_mix[i], w_in_x, gdn_conv_w[j], gdn_a_log[j], gdn_dt_bias[j],
                             gdn_o_norm[j])
            w_mix = gdn_w_o
        else:
            pre = _sc_mixer(x, norm_mix[i], sc_w_in, j, sc_conv_w[j])
            w_mix = sc_w_o
        x = _tail(x, pre, w_mix, j, norm_mem[i], xa_w_q, kv_all, i, xa_w_o)
        x = _mlp(x.reshape(b * s, d), norm_mlp[i], mlp_w1, mlp_w2, i,
                 final_norm, i == depth - 1).reshape(b, s, d)
    return x
```

```python
import functools

import jax
import jax.numpy as jnp
from jax import lax
from jax.experimental import pallas as pl
from jax.experimental.pallas import tpu as pltpu

F32 = jnp.float32
BF16 = jnp.bfloat16

EPS = 1e-6
ROPE_THETA = 10000.0
LOG2E = 1.4426950408889634
N_MIXERS = 3

MLA_HEADS = 8
MLA_NOPE = 128
MLA_ROPE = 64
MLA_V = 128
MLA_Q_RANK = 384
MLA_KV_RANK = 256

GDN_HEADS = 8
GDN_DK = 128
GDN_DV = 128
GDN_CONV = 4
GDN_CHUNK = 64

SC_CONV = 3
X_HEADS = 4

LANES = 128
SUBLANES = 8
VMEM_LIMIT_BYTES = 56 * 1024 * 1024

TM_TABLE = 1024
TM_MEM = 1024
TM_TAIL = 1024
TM_MLP = 512
TM_MLA_PROJ = 1024
TQ_MLA = 256
TK_MLA = 256
MLA_SCORE_LOOKAHEAD = 1
TM_SC = 1024
TM_GDN_PROJ = 256
T_GDN = 128
GDN_SEQS = 2
GDN_PROJ_COLS = 256
SC_COLS = 256
FF_CHUNK = 1024

NT_DIMS = (((1,), (1,)), ((), ()))
TN_DIMS = (((0,), (0,)), ((), ()))


def _tile(extent, tile):
    assert extent % tile == 0, (extent, tile)
    return tile


def _params(*sem):
    return pltpu.CompilerParams(dimension_semantics=sem, vmem_limit_bytes=VMEM_LIMIT_BYTES)


def _resident(shape):
    nd = len(shape)
    return pl.BlockSpec(shape, lambda *_: (0,) * nd, pipeline_mode=pl.Buffered(1))


def _layer_resident(shape, layer):
    nd = len(shape)
    return pl.BlockSpec((None,) + shape, lambda *_: (layer,) + (0,) * nd, pipeline_mode=pl.Buffered(1))


def _rms(x, g):
    return x * lax.rsqrt(jnp.mean(x * x, axis=-1, keepdims=True) + EPS) * g


def _dot(a, b):
    return jnp.dot(a, b, preferred_element_type=F32)


def _sigmoid(x):
    return 1.0 / (1.0 + jnp.exp2(x * -LOG2E))


def _silu(x):
    return x * _sigmoid(x)


def _rope_table_kernel(pos_ref, freq_ref, cos_ref, sin_ref):
    ang = pos_ref[...] * freq_ref[...]
    cos_ref[...] = jnp.cos(ang)
    sin_ref[...] = jnp.sin(ang)


def _rope_tables(positions):
    n = positions.size
    nf = MLA_ROPE // 2
    per = LANES // nf
    rows = n // per
    pos = jnp.repeat(positions.reshape(rows, per).astype(F32), nf, axis=1)
    inv_freq = ROPE_THETA ** (-jnp.arange(0, MLA_ROPE, 2, dtype=F32) / MLA_ROPE)
    freq = jnp.tile(inv_freq, per)[None, :]
    tm = _tile(rows, min(TM_TABLE, rows))
    row = pl.BlockSpec((tm, LANES), lambda i: (i, 0))
    cos_p, sin_p = pl.pallas_call(
        _rope_table_kernel,
        grid=(rows // tm,),
        in_specs=[row, pl.BlockSpec((1, LANES), lambda i: (0, 0))],
        out_specs=[row, row],
        out_shape=[jax.ShapeDtypeStruct((rows, LANES), F32)] * 2,
        compiler_params=_params("parallel"),
        name="rope_tables",
    )(pos, freq)
    unpack = lambda t: jnp.tile(t.reshape(n, nf), (1, per))
    return unpack(cos_p), unpack(sin_p)


def _mem_kv_kernel(mem_ref, g_ref, w_ref, out_ref):
    mn = _rms(mem_ref[...], g_ref[...]).astype(BF16)
    out_ref[0] = _dot(mn, w_ref[0].astype(BF16)).astype(BF16)


def _mem_kv(mem2d, mem_norm, w_kv):
    n, d = mem2d.shape
    depth, _, d2 = w_kv.shape
    tm = _tile(n, min(TM_MEM, n))
    return pl.pallas_call(
        _mem_kv_kernel,
        grid=(depth, n // tm),
        in_specs=[
            pl.BlockSpec((tm, d), lambda l, i: (i, 0)),
            pl.BlockSpec((1, d), lambda l, i: (0, 0)),
            pl.BlockSpec((1, d, d2), lambda l, i: (l, 0, 0)),
        ],
        out_specs=pl.BlockSpec((1, tm, d2), lambda l, i: (l, i, 0)),
        out_shape=jax.ShapeDtypeStruct((depth, n, d2), BF16),
        compiler_params=_params("parallel", "parallel"),
        name="mem_kv",
    )(mem2d, mem_norm[None, :], w_kv)


def _tail_kernel(x_ref, pre_ref, wmix_ref, g_ref, wq_ref, kv_ref, wo_ref, out_ref):
    d = x_ref.shape[-1]
    dh = d // X_HEADS
    x = x_ref[...] + _dot(pre_ref[...], wmix_ref[...].astype(BF16))
    h = _rms(x, g_ref[...]).astype(BF16)
    q = (_dot(h, wq_ref[...].astype(BF16)) * (dh ** -0.5 * LOG2E)).astype(BF16)
    scores = [lax.dot_general(q[:, hd * dh:(hd + 1) * dh], kv_ref[:, hd * dh:(hd + 1) * dh], NT_DIMS,
                              preferred_element_type=F32) for hd in range(X_HEADS)]
    heads = []
    for hd, s in enumerate(scores):
        p = jnp.exp2(s - jnp.max(s, axis=-1, keepdims=True))
        l = jnp.sum(p, axis=-1, keepdims=True)
        heads.append((_dot(p.astype(BF16), kv_ref[:, d + hd * dh:d + (hd + 1) * dh]) / l).astype(BF16))
    o = jnp.concatenate(heads, axis=-1)
    out_ref[...] = x + _dot(o, wo_ref[...].astype(BF16))


def _tail(x, pre, w_mix, mix_layer, g_mem, w_q, kv_all, layer, w_o):
    b, s, d = x.shape
    m = kv_all.shape[2]
    tm = _tile(s, TM_TAIL)
    row = pl.BlockSpec((None, tm, d), lambda bi, i: (bi, i, 0))
    return pl.pallas_call(
        _tail_kernel,
        grid=(b, s // tm),
        in_specs=[
            row, row, _layer_resident((d, d), mix_layer),
            pl.BlockSpec((1, d), lambda bi, i: (0, 0)),
            _layer_resident((d, d), layer),
            pl.BlockSpec((None, None, m, 2 * d), lambda bi, i: (layer, bi, 0, 0)),
            _layer_resident((d, d), layer),
        ],
        out_specs=row,
        out_shape=jax.ShapeDtypeStruct((b, s, d), F32),
        compiler_params=_params("parallel", "parallel"),
        name="tail",
    )(x, pre, w_mix, g_mem[None, :], w_q, kv_all, w_o)


def _mlp_kernel(x_ref, g_ref, w1_ref, w2_ref, gf_ref, out_ref, *, final):
    x = x_ref[...]
    h = _rms(x, g_ref[...]).astype(BF16)
    acc = x
    up = lambda c: _dot(h, w1_ref[:, c:c + FF_CHUNK].astype(BF16))
    ff = w1_ref.shape[1]
    a_next = up(0)
    for c in range(0, ff, FF_CHUNK):
        a = jnp.maximum(a_next, 0.0)
        if c + FF_CHUNK < ff:
            a_next = up(c + FF_CHUNK)
        acc = acc + _dot((a * a).astype(BF16), w2_ref[c:c + FF_CHUNK, :].astype(BF16))
    if final:
        acc = _rms(acc, gf_ref[...])
    out_ref[...] = acc


def _mlp(x2d, g, w1, w2, layer, g_final, final):
    n, d = x2d.shape
    ff = w1.shape[2]
    tm = _tile(n, TM_MLP)
    row = pl.BlockSpec((tm, d), lambda i: (i, 0))
    vec = pl.BlockSpec((1, d), lambda i: (0, 0))
    return pl.pallas_call(
        functools.partial(_mlp_kernel, final=final),
        grid=(n // tm,),
        in_specs=[row, vec, _layer_resident((d, ff), layer), _layer_resident((ff, d), layer), vec],
        out_specs=row,
        out_shape=jax.ShapeDtypeStruct((n, d), F32),
        compiler_params=_params("parallel"),
        name="mlp",
    )(x2d, g[None, :], w1, w2, g_final[None, :])


def _mla_weights(w_in, w_uq, w_ukv):
    half = MLA_ROPE // 2

    def swap_cols(w):
        return jnp.concatenate([-w[..., half:], w[..., :half]], axis=-1)

    lat = MLA_Q_RANK + MLA_KV_RANK
    kr = w_in[:, lat:]
    w_in_x = jnp.concatenate([w_in[:, :lat], kr, kr, swap_cols(kr), swap_cols(kr)], axis=1)
    uq = w_uq.reshape(MLA_Q_RANK, MLA_HEADS, MLA_NOPE + MLA_ROPE)
    qn = uq[:, :, :MLA_NOPE].reshape(MLA_Q_RANK, -1)
    qr = uq[:, :, MLA_NOPE:]
    w_uq_x = jnp.concatenate(
        [qn, qr.reshape(MLA_Q_RANK, -1), swap_cols(qr).reshape(MLA_Q_RANK, -1)], axis=1)
    ukv = w_ukv.reshape(MLA_KV_RANK, MLA_HEADS, MLA_NOPE + MLA_V)
    w_kn_t = ukv[:, :, :MLA_NOPE].reshape(MLA_KV_RANK, -1).T
    w_v = ukv[:, :, MLA_NOPE:].reshape(MLA_KV_RANK, -1)
    return w_in_x.astype(BF16), w_uq_x.astype(BF16), w_kn_t.astype(BF16), w_v.astype(BF16)


def _mla_proj_kernel(x_ref, g_ref, win_ref, gq_ref, gkv_ref, wuq_ref, wknt_ref, wv_ref, cos_ref, sin_ref,
                     qn_ref, qr_ref, knt_ref, krt_ref, v_ref):
    hn = MLA_HEADS * MLA_NOPE
    hr = MLA_HEADS * MLA_ROPE
    lat = MLA_Q_RANK + MLA_KV_RANK
    scale = (MLA_NOPE + MLA_ROPE) ** -0.5 * LOG2E
    h = _rms(x_ref[...], g_ref[...]).astype(BF16)
    z = _dot(h, win_ref[...])
    cos = cos_ref[...]
    sin = sin_ref[...]
    kr_t = (z[:, lat:lat + LANES] * cos + z[:, lat + LANES:lat + 2 * LANES] * sin).T
    low = lax.broadcasted_iota(jnp.int32, kr_t.shape, 0) < MLA_ROPE
    krt_ref[0:LANES, :] = jnp.where(low, kr_t, 0.0).astype(BF16)
    krt_ref[LANES:2 * LANES, :] = jnp.where(low, 0.0, kr_t).astype(BF16)
    cq = _rms(z[:, :MLA_Q_RANK], gq_ref[...]).astype(BF16)
    ckv = _rms(z[:, MLA_Q_RANK:lat], gkv_ref[...])
    q = _dot(cq, wuq_ref[...]) * scale
    qn_ref[...] = q[:, :hn].astype(BF16)
    reps = hr // LANES
    cos_r = jnp.concatenate([cos] * reps, axis=-1)
    sin_r = jnp.concatenate([sin] * reps, axis=-1)
    qr_ref[...] = (q[:, hn:hn + hr] * cos_r + q[:, hn + hr:hn + 2 * hr] * sin_r).astype(BF16)
    knt_ref[...] = _dot(wknt_ref[...], ckv.T.astype(BF16)).astype(BF16)
    v_ref[...] = _dot(ckv.astype(BF16), wv_ref[...]).astype(BF16)


def _mla_attn_kernel(qn_ref, qr_ref, knt_ref, krt_ref, v_ref, o_ref):
    s_len = qn_ref.shape[0]
    tq = TQ_MLA
    tk = TK_MLA
    rows = lax.broadcasted_iota(jnp.int32, (tq, tk), 0)
    cols = lax.broadcasted_iota(jnp.int32, (tq, tk), 1)
    lanes = lambda hl: slice(hl * LANES, (hl + 1) * LANES)

    def scores(q0, hl):
        q1 = q0 + tq
        d0 = q0 // tk * tk
        q = jnp.concatenate([qn_ref[q0:q1, lanes(hl)], qr_ref[q0:q1, :]], axis=-1)
        k_diag = jnp.concatenate([knt_ref[lanes(hl), d0:d0 + tk], krt_ref[lanes(hl), d0:d0 + tk]], axis=0)
        s_diag = jnp.where(rows + (q0 - d0) >= cols, _dot(q, k_diag), -jnp.inf)
        if not d0:
            return s_diag, None
        k_past = jnp.concatenate([knt_ref[lanes(hl), 0:d0], krt_ref[lanes(hl), 0:d0]], axis=0)
        return s_diag, _dot(q, k_past)

    def softmax(unit, s_diag, s_past):
        m = jnp.max(s_diag, axis=-1, keepdims=True)
        if s_past is not None:
            m = jnp.maximum(m, jnp.max(s_past, axis=-1, keepdims=True))
        p_diag = jnp.exp2(s_diag - m)
        l = jnp.sum(p_diag, axis=-1, keepdims=True)
        p_past = None
        if s_past is not None:
            p_past = jnp.exp2(s_past - m)
            l = l + jnp.sum(p_past, axis=-1, keepdims=True)
            p_past = p_past.astype(BF16)
        return unit, p_diag.astype(BF16), p_past, l

    def values(unit, p_diag, p_past, l):
        q0, hl = unit
        d0 = q0 // tk * tk
        acc = _dot(p_diag, v_ref[d0:d0 + tk, :])
        if p_past is not None:
            acc = acc + _dot(p_past, v_ref[0:d0, :])
        o_ref[q0:q0 + tq, lanes(hl)] = (acc[:, lanes(hl)] / l).astype(BF16)

    units = [(q0, hl) for q0 in range(0, s_len, tq) for hl in range(2)]
    ahead = MLA_SCORE_LOOKAHEAD
    queue = [scores(*u) for u in units[:ahead]]
    pending = None
    for idx, unit in enumerate(units):
        cur = queue.pop(0)
        if idx + ahead < len(units):
            queue.append(scores(*units[idx + ahead]))
        if pending is not None:
            values(*pending)
        pending = softmax(unit, *cur)
    values(*pending)


def _mla_mixer(x, g, w_in_x, gq, gkv, w_uq_x, w_kn_t, w_v, cos_t, sin_t):
    b, s, d = x.shape
    hn = MLA_HEADS * MLA_NOPE
    hr = MLA_HEADS * MLA_ROPE
    tm = _tile(s, TM_MLA_PROJ)
    _tile(s, TK_MLA)
    row = lambda w: pl.BlockSpec((None, tm, w), lambda bi, i: (bi, i, 0))
    col = lambda r: pl.BlockSpec((None, r, tm), lambda bi, i: (bi, 0, i))
    vec = lambda w: pl.BlockSpec((1, w), lambda bi, i: (0, 0))
    qn, qr, knt, krt, v = pl.pallas_call(
        _mla_proj_kernel,
        grid=(b, s // tm),
        in_specs=[row(d), vec(d), _resident(w_in_x.shape), vec(MLA_Q_RANK), vec(MLA_KV_RANK),
                  _resident(w_uq_x.shape), _resident(w_kn_t.shape), _resident(w_v.shape),
                  row(LANES), row(LANES)],
        out_specs=[row(hn), row(hr), col(hn), col(2 * LANES), row(hn)],
        out_shape=[jax.ShapeDtypeStruct((b, s, hn), BF16), jax.ShapeDtypeStruct((b, s, hr), BF16),
                   jax.ShapeDtypeStruct((b, hn, s), BF16), jax.ShapeDtypeStruct((b, 2 * LANES, s), BF16),
                   jax.ShapeDtypeStruct((b, s, hn), BF16)],
        compiler_params=_params("parallel", "parallel"),
        name="mla_proj",
    )(x, g[None, :], w_in_x, gq[None, :], gkv[None, :], w_uq_x, w_kn_t, w_v,
      cos_t.reshape(b, s, LANES), sin_t.reshape(b, s, LANES))

    pair = pl.BlockSpec((None, s, 2 * LANES), lambda bi, hp: (bi, 0, hp))
    return pl.pallas_call(
        _mla_attn_kernel,
        grid=(b, MLA_HEADS // 2),
        in_specs=[pair,
                  pl.BlockSpec((None, s, LANES), lambda bi, hp: (bi, 0, hp)),
                  pl.BlockSpec((None, 2 * LANES, s), lambda bi, hp: (bi, hp, 0)),
                  pl.BlockSpec((None, 2 * LANES, s), lambda bi, hp: (bi, 0, 0)),
                  pair],
        out_specs=pair,
        out_shape=jax.ShapeDtypeStruct((b, s, hn), BF16),
        compiler_params=_params("parallel", "parallel"),
        name="mla_attn",
    )(qn, qr, knt, krt, v)


def _zero_at_sequence_start(carry_ref):
    @pl.when(pl.program_id(1) == 0)
    def _():
        carry_ref[...] = jnp.zeros(carry_ref.shape, F32)


def _causal_conv(z, w_ref, col, carry_ref):
    tm, c = z.shape
    k = w_ref.shape[0]
    head = jnp.concatenate([carry_ref[...], z[0:SUBLANES, :]], axis=0)
    y = w_ref[k - 1:k, col:col + c] * z
    y_head = y[0:SUBLANES, :]
    for j in range(1, k):
        tap = w_ref[k - 1 - j:k - j, col:col + c]
        y = y + tap * pltpu.roll(z, j, 0)
        y_head = y_head + tap * pltpu.roll(head, j, 0)[SUBLANES:, :]
    carry_ref[...] = z[tm - SUBLANES:, :]
    return jnp.concatenate([y_head, y[SUBLANES:, :]], axis=0)


def _sc_kernel(x_ref, g_ref, win_ref, cw_ref, out_ref, buf_ref):
    w = out_ref.shape[-1]
    _zero_at_sequence_start(buf_ref)
    h = _rms(x_ref[...], g_ref[...]).astype(BF16)
    cb = SC_COLS

    def projections(c0):
        return [_dot(h, win_ref[:, part * w + c0:part * w + c0 + cb].astype(BF16)) for part in range(3)]

    nxt = projections(0)
    for c0 in range(0, w, cb):
        zb, zc, zu = nxt
        if c0 + cb < w:
            nxt = projections(c0 + cb)
        y = _causal_conv(zc * zu, cw_ref, c0, buf_ref.at[:, c0:c0 + cb])
        out_ref[:, c0:c0 + cb] = (zb * y).astype(BF16)


def _sc_mixer(x, g, w_in, layer, conv_w):
    b, s, d = x.shape
    w = conv_w.shape[1]
    tm = _tile(s, TM_SC)
    row = lambda c: pl.BlockSpec((None, tm, c), lambda bi, i: (bi, i, 0))
    return pl.pallas_call(
        _sc_kernel,
        grid=(b, s // tm),
        in_specs=[row(d), pl.BlockSpec((1, d), lambda bi, i: (0, 0)),
                  _layer_resident(w_in.shape[1:], layer),
                  pl.BlockSpec(conv_w.shape, lambda bi, i: (0, 0))],
        out_specs=row(w),
        out_shape=jax.ShapeDtypeStruct((b, s, w), BF16),
        scratch_shapes=[pltpu.VMEM((SUBLANES, w), F32)],
        compiler_params=_params("parallel", "arbitrary"),
        name="short_conv",
    )(x, g[None, :], w_in, conv_w)


def _gdn_proj_kernel(x_ref, g_ref, win_ref, cw_ref, alog_ref, dtb_ref,
                     q_ref, k_ref, v_ref, gate_ref, bg_ref, buf_ref):
    hk = GDN_HEADS * GDN_DK
    _zero_at_sequence_start(buf_ref)
    h = _rms(x_ref[...], g_ref[...]).astype(BF16)
    cb = GDN_PROJ_COLS
    proj = lambda c0: _dot(h, win_ref[:, c0:c0 + cb])
    gate_cols = list(range(0, hk, cb))
    z_next = proj(0)
    for c0 in range(0, 3 * hk, cb):
        z = z_next
        if c0 + cb < 3 * hk:
            z_next = proj(c0 + cb)
        if gate_cols:
            g0 = gate_cols.pop(0)
            gate_ref[:, g0:g0 + cb] = proj(3 * hk + g0).astype(BF16)
        part, off = divmod(c0, hk)
        ref = (q_ref, k_ref, v_ref)[part]
        y = _silu(_causal_conv(z, cw_ref, c0, buf_ref.at[:, c0:c0 + cb]))
        if part < 2:
            post = GDN_DK ** -0.5 if part == 0 else 1.0
            for d0 in range(0, cb, GDN_DK):
                yh = y[:, d0:d0 + GDN_DK]
                inv = lax.rsqrt(jnp.sum(yh * yh, axis=-1, keepdims=True) + EPS) * post
                ref[:, off + d0:off + d0 + GDN_DK] = (yh * inv).astype(BF16)
        else:
            ref[:, off:off + cb] = y.astype(BF16)
    for g0 in gate_cols:
        gate_ref[:, g0:g0 + cb] = proj(3 * hk + g0).astype(BF16)
    zb = _dot(h, win_ref[:, 4 * hk:4 * hk + LANES])
    beta = jax.nn.sigmoid(zb)
    t = zb + dtb_ref[...]
    softplus = jnp.maximum(t, 0.0) + jnp.log1p(jnp.exp(-jnp.abs(t)))
    decay = -jnp.exp(alog_ref[...]) * softplus
    lane = lax.broadcasted_iota(jnp.int32, zb.shape, 1)
    bg_ref[...] = jnp.where(lane < GDN_HEADS, beta, decay)


def _cumsum_rows(x):
    n = x.shape[0]
    row = lax.broadcasted_iota(jnp.int32, x.shape, 0)
    shift = 1
    while shift < n:
        x = x + jnp.where(row >= shift, pltpu.roll(x, shift, 0), 0.0)
        shift *= 2
    return x


def _unit_lower_inverses(ms):
    c = ms[0].shape[0]
    eye = (lax.broadcasted_iota(jnp.int32, (c, c), 0) == lax.broadcasted_iota(jnp.int32, (c, c), 1)).astype(F32)
    invs = [eye - m for m in ms]
    powers = [m.astype(BF16) for m in ms]
    width = 2
    while width < c:
        powers = [_dot(p, p).astype(BF16) for p in powers]
        invs = [inv + _dot(inv.astype(BF16), p) for inv, p in zip(invs, powers)]
        width *= 2
    return invs


def _gdn_chunk_kernel(q_ref, k_ref, v_ref, gate_ref, bg_ref, onorm_ref, out_ref, state_ref):
    c = GDN_CHUNK
    nb, t = q_ref.shape[0], q_ref.shape[1]
    nh = GDN_HEADS

    @pl.when(pl.program_id(1) == 0)
    def _():
        state_ref[...] = jnp.zeros(state_ref.shape, F32)

    rows = lax.broadcasted_iota(jnp.int32, (c, c), 0)
    cols = lax.broadcasted_iota(jnp.int32, (c, c), 1)
    tri = rows >= cols
    strict = rows > cols
    onorm = onorm_ref[...]
    sl = [slice(hd * GDN_DK, (hd + 1) * GDN_DK) for hd in range(nh)]

    def intra(chunks):
        items = [(bi, r0, hd) for bi, r0 in chunks for hd in range(nh)]
        bgs = {ch: bg_ref[ch[0], ch[1]:ch[1] + c, :] for ch in chunks}
        gcums = {ch: _cumsum_rows(bgs[ch]) for ch in chunks}
        gcums_t = {ch: gcums[ch].T for ch in chunks}
        egs = {ch: jnp.exp(gcums[ch]) for ch in chunks}
        eds = {ch: jnp.exp(gcums[ch][c - 1:c, :] - gcums[ch]) for ch in chunks}
        column = lambda a, lane: jnp.broadcast_to(a[:, lane:lane + 1], (c, GDN_DK))
        qbf = [q_ref[bi, r0:r0 + c, sl[hd]] for bi, r0, hd in items]
        kbf = [k_ref[bi, r0:r0 + c, sl[hd]] for bi, r0, hd in items]
        vbf = [v_ref[bi, r0:r0 + c, sl[hd]] for bi, r0, hd in items]
        scaled = lambda xs, fs: [(x.astype(F32) * f).astype(BF16) for x, f in zip(xs, fs)]
        beta = [column(bgs[bi, r0], hd) for bi, r0, hd in items]
        eg = [column(egs[bi, r0], nh + hd) for bi, r0, hd in items]
        kb = scaled(kbf, beta)
        vb = scaled(vbf, beta)
        kbe = scaled(kbf, [b * e for b, e in zip(beta, eg)])
        qe = scaled(qbf, eg)
        kd = scaled(kbf, [column(eds[bi, r0], nh + hd) for bi, r0, hd in items])
        gcol = [gcums[bi, r0][:, nh + hd:nh + hd + 1] for bi, r0, hd in items]
        grow = [gcums_t[bi, r0][nh + hd:nh + hd + 1, :] for bi, r0, hd in items]
        decay = [jnp.where(tri, jnp.exp(jnp.where(tri, gc - gr, 0.0)), 0.0) for gc, gr in zip(gcol, grow)]
        kk = [lax.dot_general(a, b, NT_DIMS, preferred_element_type=F32) for a, b in zip(kb, kbf)]
        attn = [(lax.dot_general(a, b, NT_DIMS, preferred_element_type=F32) * d).astype(BF16)
                for a, b, d in zip(qbf, kbf, decay)]
        t_inv = [x.astype(BF16) for x in
                 _unit_lower_inverses([jnp.where(strict, a * d, 0.0) for a, d in zip(kk, decay)])]
        u = [_dot(ti, x) for ti, x in zip(t_inv, vb)]
        w = [_dot(ti, x).astype(BF16) for ti, x in zip(t_inv, kbe)]
        e_last = [egs[bi, r0][c - 1:c, nh + hd:nh + hd + 1] for bi, r0, hd in items]
        fields = (u, w, attn, qe, kd, e_last)
        return {ch: tuple(f[i * nh:(i + 1) * nh] for f in fields) for i, ch in enumerate(chunks)}

    def recur(r0, pres, state):
        u, w, attn, qe, kd, e_last = (sum((list(p[f]) for p in pres), []) for f in range(6))
        sb = [x.astype(BF16) for x in state]
        vnb = [(a - _dot(b, s)).astype(BF16) for a, b, s in zip(u, w, sb)]
        o = [_dot(x, s) + _dot(a, vn) for x, s, a, vn in zip(qe, sb, attn, vnb)]
        state = [s * e + lax.dot_general(x, vn, TN_DIMS, preferred_element_type=F32)
                 for s, e, x, vn in zip(state, e_last, kd, vnb)]
        for bi in range(nb):
            gated = [(_rms(o[bi * nh + hd], onorm)
                      * _silu(gate_ref[bi, r0:r0 + c, sl[hd]].astype(F32))).astype(BF16) for hd in range(nh)]
            out_ref[bi, r0:r0 + c, :] = jnp.concatenate(gated, axis=-1)
        return state

    starts = list(range(0, t, c))
    pre = intra([(bi, r0) for r0 in starts for bi in range(nb)])
    state = [state_ref[i] for i in range(nb * nh)]
    for r0 in starts:
        state = recur(r0, [pre[bi, r0] for bi in range(nb)], state)
    for i in range(nb * nh):
        state_ref[i] = state[i]


def _gdn_mixer(x, g, w_in_x, conv_w, a_log, dt_bias, o_norm):
    b, s, d = x.shape
    hk = GDN_HEADS * GDN_DK
    lanes_pad = jnp.zeros((LANES - 2 * GDN_HEADS,), F32)
    zeros_h = jnp.zeros((GDN_HEADS,), F32)
    alog_row = jnp.concatenate([zeros_h, a_log, lanes_pad])[None, :]
    dtb_row = jnp.concatenate([zeros_h, dt_bias, lanes_pad])[None, :]
    tm = _tile(s, TM_GDN_PROJ)
    row = lambda tt, c: pl.BlockSpec((None, tt, c), lambda bi, i: (bi, i, 0))
    vec = lambda c: pl.BlockSpec((1, c), lambda bi, i: (0, 0))
    q, k, v, gate, bg = pl.pallas_call(
        _gdn_proj_kernel,
        grid=(b, s // tm),
        in_specs=[row(tm, d), vec(d), _resident(w_in_x.shape),
                  pl.BlockSpec(conv_w.shape, lambda bi, i: (0, 0)), vec(LANES), vec(LANES)],
        out_specs=[row(tm, hk), row(tm, hk), row(tm, hk), row(tm, hk), row(tm, LANES)],
        out_shape=[jax.ShapeDtypeStruct((b, s, hk), BF16)] * 4 + [jax.ShapeDtypeStruct((b, s, LANES), F32)],
        scratch_shapes=[pltpu.VMEM((SUBLANES, 3 * hk), F32)],
        compiler_params=_params("parallel", "arbitrary"),
        name="gdn_proj",
    )(x, g[None, :], w_in_x, conv_w, alog_row, dtb_row)

    tt = _tile(s, T_GDN)
    nb = _tile(b, GDN_SEQS)
    seqs = lambda c: pl.BlockSpec((nb, tt, c), lambda bi, i: (bi, i, 0))
    return pl.pallas_call(
        _gdn_chunk_kernel,
        grid=(b // nb, s // tt),
        in_specs=[seqs(hk), seqs(hk), seqs(hk), seqs(hk), seqs(LANES), vec(GDN_DV)],
        out_specs=seqs(hk),
        out_shape=jax.ShapeDtypeStruct((b, s, hk), BF16),
        scratch_shapes=[pltpu.VMEM((nb * GDN_HEADS, GDN_DK, GDN_DV), F32)],
        compiler_params=_params("parallel", "arbitrary"),
        name="gdn_chunk",
    )(q, k, v, gate, bg, o_norm[None, :])


def kernel(x, mem, positions, mla_w_in, mla_q_norm, mla_kv_norm, mla_w_uq, mla_w_ukv, mla_w_o, gdn_w_in, gdn_conv_w, gdn_a_log, gdn_dt_bias, gdn_o_norm, gdn_w_o, sc_w_in, sc_conv_w, sc_w_o, norm_mix, norm_mem, norm_mlp, xa_w_q, xa_w_kv, xa_w_o, mlp_w1, mlp_w2, mem_norm, final_norm):
    b, s, d = x.shape
    depth = norm_mix.shape[0]
    m = mem.shape[1]
    cos_t, sin_t = _rope_tables(positions)
    kv_all = _mem_kv(mem.reshape(b * m, d), mem_norm, xa_w_kv).reshape(depth, b, m, 2 * d)
    for i in range(depth):
        j = i // N_MIXERS
        kind = i % N_MIXERS
        if kind == 0:
            w_in_x, w_uq_x, w_kn_t, w_v = _mla_weights(mla_w_in[j], mla_w_uq[j], mla_w_ukv[j])
            pre = _mla_mixer(x, norm_mix[i], w_in_x, mla_q_norm[j], mla_kv_norm[j], w_uq_x, w_kn_t, w_v,
                             cos_t, sin_t)
            w_mix = mla_w_o
        elif kind == 1:
            pad = jnp.zeros((d, LANES - 2 * GDN_HEADS), F32)
            w_in_x = jnp.concatenate([gdn_w_in[j], pad], axis=1).astype(BF16)
            pre = _gdn_mixer(x, norm_mix[i], w_in_x, gdn_conv_w[j], gdn_a_log[j], gdn_dt_bias[j],
                             gdn_o_norm[j])
            w_mix = gdn_w_o
        else:
            pre = _sc_mixer(x, norm_mix[i], sc_w_in, j, sc_conv_w[j])
            w_mix = sc_w_o
        x = _tail(x, pre, w_mix, j, norm_mem[i], xa_w_q, kv_all, i, xa_w_o)
        x = _mlp(x.reshape(b * s, d), norm_mlp[i], mlp_w1, mlp_w2, i,
                 final_norm, i == depth - 1).reshape(b, s, d)
    return x
```

```python
import functools

import jax
import jax.numpy as jnp
from jax import lax
from jax.experimental import pallas as pl
from jax.experimental.pallas import tpu as pltpu

F32 = jnp.float32
BF16 = jnp.bfloat16

EPS = 1e-6
ROPE_THETA = 10000.0
LOG2E = 1.4426950408889634
N_MIXERS = 3

MLA_HEADS = 8
MLA_NOPE = 128
MLA_ROPE = 64
MLA_V = 128
MLA_Q_RANK = 384
MLA_KV_RANK = 256

GDN_HEADS = 8
GDN_DK = 128
GDN_DV = 128
GDN_CONV = 4
GDN_CHUNK = 64

SC_CONV = 3
X_HEADS = 4

LANES = 128
SUBLANES = 8
VMEM_LIMIT_BYTES = 56 * 1024 * 1024

TM_TABLE = 1024
TM_MEM = 1024
TM_TAIL = 1024
TM_MLP = 512
TM_MLA_PROJ = 1024
TQ_MLA = 256
TK_MLA = 256
MLA_SCORE_LOOKAHEAD = 1
TM_SC = 1024
TM_GDN_PROJ = 256
T_GDN = 128
GDN_SEQS = 2
GDN_PROJ_COLS = 256
SC_COLS = 256
FF_CHUNK = 1024

NT_DIMS = (((1,), (1,)), ((), ()))
TN_DIMS = (((0,), (0,)), ((), ()))


def _tile(extent, tile):
    assert extent % tile == 0, (extent, tile)
    return tile


def _params(*sem):
    return pltpu.CompilerParams(dimension_semantics=sem, vmem_limit_bytes=VMEM_LIMIT_BYTES)


def _resident(shape):
    nd = len(shape)
    return pl.BlockSpec(shape, lambda *_: (0,) * nd, pipeline_mode=pl.Buffered(1))


def _layer_resident(shape, layer):
    nd = len(shape)
    return pl.BlockSpec((None,) + shape, lambda *_: (layer,) + (0,) * nd, pipeline_mode=pl.Buffered(1))


def _rms(x, g):
    return x * lax.rsqrt(jnp.mean(x * x, axis=-1, keepdims=True) + EPS) * g


def _dot(a, b):
    return jnp.dot(a, b, preferred_element_type=F32)


def _sigmoid(x):
    return 1.0 / (1.0 + jnp.exp2(x * -LOG2E))


def _silu(x):
    return x * _sigmoid(x)


def _rope_table_kernel(pos_ref, freq_ref, cos_ref, sin_ref):
    ang = pos_ref[...] * freq_ref[...]
    cos_ref[...] = jnp.cos(ang)
    sin_ref[...] = jnp.sin(ang)


def _rope_tables(positions):
    n = positions.size
    pos = jnp.broadcast_to(positions.reshape(n, 1).astype(F32), (n, LANES))
    inv_freq = ROPE_THETA ** (-jnp.arange(0, MLA_ROPE, 2, dtype=F32) / MLA_ROPE)
    freq = jnp.tile(inv_freq, 2 * LANES // MLA_ROPE)[None, :]
    tm = _tile(n, TM_TABLE)
    row = pl.BlockSpec((tm, LANES), lambda i: (i, 0))
    return pl.pallas_call(
        _rope_table_kernel,
        grid=(n // tm,),
        in_specs=[row, pl.BlockSpec((1, LANES), lambda i: (0, 0))],
        out_specs=[row, row],
        out_shape=[jax.ShapeDtypeStruct((n, LANES), F32)] * 2,
        compiler_params=_params("parallel"),
        name="rope_tables",
    )(pos, freq)


def _mem_kv_kernel(mem_ref, g_ref, w_ref, out_ref):
    mn = _rms(mem_ref[...], g_ref[...]).astype(BF16)
    out_ref[0] = _dot(mn, w_ref[0].astype(BF16)).astype(BF16)


def _mem_kv(mem2d, mem_norm, w_kv):
    n, d = mem2d.shape
    depth, _, d2 = w_kv.shape
    tm = _tile(n, min(TM_MEM, n))
    return pl.pallas_call(
        _mem_kv_kernel,
        grid=(depth, n // tm),
        in_specs=[
            pl.BlockSpec((tm, d), lambda l, i: (i, 0)),
            pl.BlockSpec((1, d), lambda l, i: (0, 0)),
            pl.BlockSpec((1, d, d2), lambda l, i: (l, 0, 0)),
        ],
        out_specs=pl.BlockSpec((1, tm, d2), lambda l, i: (l, i, 0)),
        out_shape=jax.ShapeDtypeStruct((depth, n, d2), BF16),
        compiler_params=_params("parallel", "parallel"),
        name="mem_kv",
    )(mem2d, mem_norm[None, :], w_kv)


def _tail_kernel(x_ref, pre_ref, wmix_ref, g_ref, wq_ref, kv_ref, wo_ref, out_ref):
    d = x_ref.shape[-1]
    dh = d // X_HEADS
    x = x_ref[...] + _dot(pre_ref[...], wmix_ref[...].astype(BF16))
    h = _rms(x, g_ref[...]).astype(BF16)
    q = (_dot(h, wq_ref[...].astype(BF16)) * (dh ** -0.5 * LOG2E)).astype(BF16)
    scores = [lax.dot_general(q[:, hd * dh:(hd + 1) * dh], kv_ref[:, hd * dh:(hd + 1) * dh], NT_DIMS,
                              preferred_element_type=F32) for hd in range(X_HEADS)]
    heads = []
    for hd, s in enumerate(scores):
        p = jnp.exp2(s - jnp.max(s, axis=-1, keepdims=True))
        l = jnp.sum(p, axis=-1, keepdims=True)
        heads.append((_dot(p.astype(BF16), kv_ref[:, d + hd * dh:d + (hd + 1) * dh]) / l).astype(BF16))
    o = jnp.concatenate(heads, axis=-1)
    out_ref[...] = x + _dot(o, wo_ref[...].astype(BF16))


def _tail(x, pre, w_mix, mix_layer, g_mem, w_q, kv_all, layer, w_o):
    b, s, d = x.shape
    m = kv_all.shape[2]
    tm = _tile(s, TM_TAIL)
    row = pl.BlockSpec((None, tm, d), lambda bi, i: (bi, i, 0))
    return pl.pallas_call(
        _tail_kernel,
        grid=(b, s // tm),
        in_specs=[
            row, row, _layer_resident((d, d), mix_layer),
            pl.BlockSpec((1, d), lambda bi, i: (0, 0)),
            _layer_resident((d, d), layer),
            pl.BlockSpec((None, None, m, 2 * d), lambda bi, i: (layer, bi, 0, 0)),
            _layer_resident((d, d), layer),
        ],
        out_specs=row,
        out_shape=jax.ShapeDtypeStruct((b, s, d), F32),
        compiler_params=_params("parallel", "parallel"),
        name="tail",
    )(x, pre, w_mix, g_mem[None, :], w_q, kv_all, w_o)


def _mlp_kernel(x_ref, g_ref, w1_ref, w2_ref, gf_ref, out_ref, *, final):
    x = x_ref[...]
    h = _rms(x, g_ref[...]).astype(BF16)
    acc = x
    up = lambda c: _dot(h, w1_ref[:, c:c + FF_CHUNK].astype(BF16))
    ff = w1_ref.shape[1]
    a_next = up(0)
    for c in range(0, ff, FF_CHUNK):
        a = jnp.maximum(a_next, 0.0)
        if c + FF_CHUNK < ff:
            a_next = up(c + FF_CHUNK)
        acc = acc + _dot((a * a).astype(BF16), w2_ref[c:c + FF_CHUNK, :].astype(BF16))
    if final:
        acc = _rms(acc, gf_ref[...])
    out_ref[...] = acc


def _mlp(x2d, g, w1, w2, layer, g_final, final):
    n, d = x2d.shape
    ff = w1.shape[2]
    tm = _tile(n, TM_MLP)
    row = pl.BlockSpec((tm, d), lambda i: (i, 0))
    vec = pl.BlockSpec((1, d), lambda i: (0, 0))
    return pl.pallas_call(
        functools.partial(_mlp_kernel, final=final),
        grid=(n // tm,),
        in_specs=[row, vec, _layer_resident((d, ff), layer), _layer_resident((ff, d), layer), vec],
        out_specs=row,
        out_shape=jax.ShapeDtypeStruct((n, d), F32),
        compiler_params=_params("parallel"),
        name="mlp",
    )(x2d, g[None, :], w1, w2, g_final[None, :])


def _mla_weights(w_in, w_uq, w_ukv):
    half = MLA_ROPE // 2

    def swap_cols(w):
        return jnp.concatenate([-w[..., half:], w[..., :half]], axis=-1)

    lat = MLA_Q_RANK + MLA_KV_RANK
    kr = w_in[:, lat:]
    w_in_x = jnp.concatenate([w_in[:, :lat], kr, kr, swap_cols(kr), swap_cols(kr)], axis=1)
    uq = w_uq.reshape(MLA_Q_RANK, MLA_HEADS, MLA_NOPE + MLA_ROPE)
    qn = uq[:, :, :MLA_NOPE].reshape(MLA_Q_RANK, -1)
    qr = uq[:, :, MLA_NOPE:]
    w_uq_x = jnp.concatenate(
        [qn, qr.reshape(MLA_Q_RANK, -1), swap_cols(qr).reshape(MLA_Q_RANK, -1)], axis=1)
    ukv = w_ukv.reshape(MLA_KV_RANK, MLA_HEADS, MLA_NOPE + MLA_V)
    w_kn_t = ukv[:, :, :MLA_NOPE].reshape(MLA_KV_RANK, -1).T
    w_v = ukv[:, :, MLA_NOPE:].reshape(MLA_KV_RANK, -1)
    return w_in_x.astype(BF16), w_uq_x.astype(BF16), w_kn_t.astype(BF16), w_v.astype(BF16)


def _mla_proj_kernel(x_ref, g_ref, win_ref, gq_ref, gkv_ref, wuq_ref, wknt_ref, wv_ref, cos_ref, sin_ref,
                     qn_ref, qr_ref, knt_ref, krt_ref, v_ref):
    hn = MLA_HEADS * MLA_NOPE
    hr = MLA_HEADS * MLA_ROPE
    lat = MLA_Q_RANK + MLA_KV_RANK
    scale = (MLA_NOPE + MLA_ROPE) ** -0.5 * LOG2E
    h = _rms(x_ref[...], g_ref[...]).astype(BF16)
    z = _dot(h, win_ref[...])
    cos = cos_ref[...]
    sin = sin_ref[...]
    kr_t = (z[:, lat:lat + LANES] * cos + z[:, lat + LANES:lat + 2 * LANES] * sin).T
    low = lax.broadcasted_iota(jnp.int32, kr_t.shape, 0) < MLA_ROPE
    krt_ref[0:LANES, :] = jnp.where(low, kr_t, 0.0).astype(BF16)
    krt_ref[LANES:2 * LANES, :] = jnp.where(low, 0.0, kr_t).astype(BF16)
    cq = _rms(z[:, :MLA_Q_RANK], gq_ref[...]).astype(BF16)
    ckv = _rms(z[:, MLA_Q_RANK:lat], gkv_ref[...])
    q = _dot(cq, wuq_ref[...]) * scale
    qn_ref[...] = q[:, :hn].astype(BF16)
    reps = hr // LANES
    cos_r = jnp.concatenate([cos] * reps, axis=-1)
    sin_r = jnp.concatenate([sin] * reps, axis=-1)
    qr_ref[...] = (q[:, hn:hn + hr] * cos_r + q[:, hn + hr:hn + 2 * hr] * sin_r).astype(BF16)
    knt_ref[...] = _dot(wknt_ref[...], ckv.T.astype(BF16)).astype(BF16)
    v_ref[...] = _dot(ckv.astype(BF16), wv_ref[...]).astype(BF16)


def _mla_attn_kernel(qn_ref, qr_ref, knt_ref, krt_ref, v_ref, o_ref):
    s_len = qn_ref.shape[0]
    tq = TQ_MLA
    tk = TK_MLA
    rows = lax.broadcasted_iota(jnp.int32, (tq, tk), 0)
    cols = lax.broadcasted_iota(jnp.int32, (tq, tk), 1)
    lanes = lambda hl: slice(hl * LANES, (hl + 1) * LANES)

    def scores(q0, hl):
        q1 = q0 + tq
        d0 = q0 // tk * tk
        q = jnp.concatenate([qn_ref[q0:q1, lanes(hl)], qr_ref[q0:q1, :]], axis=-1)
        k_diag = jnp.concatenate([knt_ref[lanes(hl), d0:d0 + tk], krt_ref[lanes(hl), d0:d0 + tk]], axis=0)
        s_diag = jnp.where(rows + (q0 - d0) >= cols, _dot(q, k_diag), -jnp.inf)
        if not d0:
            return s_diag, None
        k_past = jnp.concatenate([knt_ref[lanes(hl), 0:d0], krt_ref[lanes(hl), 0:d0]], axis=0)
        return s_diag, _dot(q, k_past)

    def softmax(unit, s_diag, s_past):
        m = jnp.max(s_diag, axis=-1, keepdims=True)
        if s_past is not None:
            m = jnp.maximum(m, jnp.max(s_past, axis=-1, keepdims=True))
        p_diag = jnp.exp2(s_diag - m)
        l = jnp.sum(p_diag, axis=-1, keepdims=True)
        p_past = None
        if s_past is not None:
            p_past = jnp.exp2(s_past - m)
            l = l + jnp.sum(p_past, axis=-1, keepdims=True)
            p_past = p_past.astype(BF16)
        return unit, p_diag.astype(BF16), p_past, l

    def values(unit, p_diag, p_past, l):
        q0, hl = unit
        d0 = q0 // tk * tk
        acc = _dot(p_diag, v_ref[d0:d0 + tk, :])
        if p_past is not None:
            acc = acc + _dot(p_past, v_ref[0:d0, :])
        o_ref[q0:q0 + tq, lanes(hl)] = (acc[:, lanes(hl)] / l).astype(BF16)

    units = [(q0, hl) for q0 in range(0, s_len, tq) for hl in range(2)]
    ahead = MLA_SCORE_LOOKAHEAD
    queue = [scores(*u) for u in units[:ahead]]
    pending = None
    for idx, unit in enumerate(units):
        cur = queue.pop(0)
        if idx + ahead < len(units):
            queue.append(scores(*units[idx + ahead]))
        if pending is not None:
            values(*pending)
        pending = softmax(unit, *cur)
    values(*pending)


def _mla_mixer(x, g, w_in_x, gq, gkv, w_uq_x, w_kn_t, w_v, cos_t, sin_t):
    b, s, d = x.shape
    hn = MLA_HEADS * MLA_NOPE
    hr = MLA_HEADS * MLA_ROPE
    tm = _tile(s, TM_MLA_PROJ)
    _tile(s, TK_MLA)
    row = lambda w: pl.BlockSpec((None, tm, w), lambda bi, i: (bi, i, 0))
    col = lambda r: pl.BlockSpec((None, r, tm), lambda bi, i: (bi, 0, i))
    vec = lambda w: pl.BlockSpec((1, w), lambda bi, i: (0, 0))
    qn, qr, knt, krt, v = pl.pallas_call(
        _mla_proj_kernel,
        grid=(b, s // tm),
        in_specs=[row(d), vec(d), _resident(w_in_x.shape), vec(MLA_Q_RANK), vec(MLA_KV_RANK),
                  _resident(w_uq_x.shape), _resident(w_kn_t.shape), _resident(w_v.shape),
                  row(LANES), row(LANES)],
        out_specs=[row(hn), row(hr), col(hn), col(2 * LANES), row(hn)],
        out_shape=[jax.ShapeDtypeStruct((b, s, hn), BF16), jax.ShapeDtypeStruct((b, s, hr), BF16),
                   jax.ShapeDtypeStruct((b, hn, s), BF16), jax.ShapeDtypeStruct((b, 2 * LANES, s), BF16),
                   jax.ShapeDtypeStruct((b, s, hn), BF16)],
        compiler_params=_params("parallel", "parallel"),
        name="mla_proj",
    )(x, g[None, :], w_in_x, gq[None, :], gkv[None, :], w_uq_x, w_kn_t, w_v,
      cos_t.reshape(b, s, LANES), sin_t.reshape(b, s, LANES))

    pair = pl.BlockSpec((None, s, 2 * LANES), lambda bi, hp: (bi, 0, hp))
    return pl.pallas_call(
        _mla_attn_kernel,
        grid=(b, MLA_HEADS // 2),
        in_specs=[pair,
                  pl.BlockSpec((None, s, LANES), lambda bi, hp: (bi, 0, hp)),
                  pl.BlockSpec((None, 2 * LANES, s), lambda bi, hp: (bi, hp, 0)),
                  pl.BlockSpec((None, 2 * LANES, s), lambda bi, hp: (bi, 0, 0)),
                  pair],
        out_specs=pair,
        out_shape=jax.ShapeDtypeStruct((b, s, hn), BF16),
        compiler_params=_params("parallel", "parallel"),
        name="mla_attn",
    )(qn, qr, knt, krt, v)


def _zero_at_sequence_start(carry_ref):
    @pl.when(pl.program_id(1) == 0)
    def _():
        carry_ref[...] = jnp.zeros(carry_ref.shape, F32)


def _causal_conv(z, w_ref, col, carry_ref):
    tm, c = z.shape
    k = w_ref.shape[0]
    head = jnp.concatenate([carry_ref[...], z[0:SUBLANES, :]], axis=0)
    y = w_ref[k - 1:k, col:col + c] * z
    y_head = y[0:SUBLANES, :]
    for j in range(1, k):
        tap = w_ref[k - 1 - j:k - j, col:col + c]
        y = y + tap * pltpu.roll(z, j, 0)
        y_head = y_head + tap * pltpu.roll(head, j, 0)[SUBLANES:, :]
    carry_ref[...] = z[tm - SUBLANES:, :]
    return jnp.concatenate([y_head, y[SUBLANES:, :]], axis=0)


def _sc_kernel(x_ref, g_ref, win_ref, cw_ref, out_ref, buf_ref):
    w = out_ref.shape[-1]
    _zero_at_sequence_start(buf_ref)
    h = _rms(x_ref[...], g_ref[...]).astype(BF16)
    cb = SC_COLS

    def projections(c0):
        return [_dot(h, win_ref[:, part * w + c0:part * w + c0 + cb].astype(BF16)) for part in range(3)]

    nxt = projections(0)
    for c0 in range(0, w, cb):
        zb, zc, zu = nxt
        if c0 + cb < w:
            nxt = projections(c0 + cb)
        y = _causal_conv(zc * zu, cw_ref, c0, buf_ref.at[:, c0:c0 + cb])
        out_ref[:, c0:c0 + cb] = (zb * y).astype(BF16)


def _sc_mixer(x, g, w_in, layer, conv_w):
    b, s, d = x.shape
    w = conv_w.shape[1]
    tm = _tile(s, TM_SC)
    row = lambda c: pl.BlockSpec((None, tm, c), lambda bi, i: (bi, i, 0))
    return pl.pallas_call(
        _sc_kernel,
        grid=(b, s // tm),
        in_specs=[row(d), pl.BlockSpec((1, d), lambda bi, i: (0, 0)),
                  _layer_resident(w_in.shape[1:], layer),
                  pl.BlockSpec(conv_w.shape, lambda bi, i: (0, 0))],
        out_specs=row(w),
        out_shape=jax.ShapeDtypeStruct((b, s, w), BF16),
        scratch_shapes=[pltpu.VMEM((SUBLANES, w), F32)],
        compiler_params=_params("parallel", "arbitrary"),
        name="short_conv",
    )(x, g[None, :], w_in, conv_w)


def _gdn_proj_kernel(x_ref, g_ref, win_ref, cw_ref, alog_ref, dtb_ref,
                     q_ref, k_ref, v_ref, gate_ref, bg_ref, buf_ref):
    hk = GDN_HEADS * GDN_DK
    _zero_at_sequence_start(buf_ref)
    h = _rms(x_ref[...], g_ref[...]).astype(BF16)
    cb = GDN_PROJ_COLS
    proj = lambda c0: _dot(h, win_ref[:, c0:c0 + cb])
    gate_cols = list(range(0, hk, cb))
    z_next = proj(0)
    for c0 in range(0, 3 * hk, cb):
        z = z_next
        if c0 + cb < 3 * hk:
            z_next = proj(c0 + cb)
        if gate_cols:
            g0 = gate_cols.pop(0)
            gate_ref[:, g0:g0 + cb] = proj(3 * hk + g0).astype(BF16)
        part, off = divmod(c0, hk)
        ref = (q_ref, k_ref, v_ref)[part]
        y = _silu(_causal_conv(z, cw_ref, c0, buf_ref.at[:, c0:c0 + cb]))
        if part < 2:
            post = GDN_DK ** -0.5 if part == 0 else 1.0
            for d0 in range(0, cb, GDN_DK):
                yh = y[:, d0:d0 + GDN_DK]
                inv = lax.rsqrt(jnp.sum(yh * yh, axis=-1, keepdims=True) + EPS) * post
                ref[:, off + d0:off + d0 + GDN_DK] = (yh * inv).astype(BF16)
        else:
            ref[:, off:off + cb] = y.astype(BF16)
    for g0 in gate_cols:
        gate_ref[:, g0:g0 + cb] = proj(3 * hk + g0).astype(BF16)
    zb = _dot(h, win_ref[:, 4 * hk:4 * hk + LANES])
    beta = jax.nn.sigmoid(zb)
    t = zb + dtb_ref[...]
    softplus = jnp.maximum(t, 0.0) + jnp.log1p(jnp.exp(-jnp.abs(t)))
    decay = -jnp.exp(alog_ref[...]) * softplus
    lane = lax.broadcasted_iota(jnp.int32, zb.shape, 1)
    bg_ref[...] = jnp.where(lane < GDN_HEADS, beta, decay)


def _cumsum_rows(x):
    n = x.shape[0]
    row = lax.broadcasted_iota(jnp.int32, x.shape, 0)
    shift = 1
    while shift < n:
        x = x + jnp.where(row >= shift, pltpu.roll(x, shift, 0), 0.0)
        shift *= 2
    return x


def _unit_lower_inverses(ms):
    c = ms[0].shape[0]
    eye = (lax.broadcasted_iota(jnp.int32, (c, c), 0) == lax.broadcasted_iota(jnp.int32, (c, c), 1)).astype(F32)
    invs = [eye - m for m in ms]
    powers = [m.astype(BF16) for m in ms]
    width = 2
    while width < c:
        powers = [_dot(p, p).astype(BF16) for p in powers]
        invs = [inv + _dot(inv.astype(BF16), p) for inv, p in zip(invs, powers)]
        width *= 2
    return invs


def _gdn_chunk_kernel(q_ref, k_ref, v_ref, gate_ref, bg_ref, onorm_ref, out_ref, state_ref):
    c = GDN_CHUNK
    nb, t = q_ref.shape[0], q_ref.shape[1]
    nh = GDN_HEADS

    @pl.when(pl.program_id(1) == 0)
    def _():
        state_ref[...] = jnp.zeros(state_ref.shape, F32)

    rows = lax.broadcasted_iota(jnp.int32, (c, c), 0)
    cols = lax.broadcasted_iota(jnp.int32, (c, c), 1)
    tri = rows >= cols
    strict = rows > cols
    onorm = onorm_ref[...]
    sl = [slice(hd * GDN_DK, (hd + 1) * GDN_DK) for hd in range(nh)]

    def intra(chunks):
        items = [(bi, r0, hd) for bi, r0 in chunks for hd in range(nh)]
        bgs = {ch: bg_ref[ch[0], ch[1]:ch[1] + c, :] for ch in chunks}
        gcums = {ch: _cumsum_rows(bgs[ch]) for ch in chunks}
        gcums_t = {ch: gcums[ch].T for ch in chunks}
        egs = {ch: jnp.exp(gcums[ch]) for ch in chunks}
        eds = {ch: jnp.exp(gcums[ch][c - 1:c, :] - gcums[ch]) for ch in chunks}
        column = lambda a, lane: jnp.broadcast_to(a[:, lane:lane + 1], (c, GDN_DK))
        qbf = [q_ref[bi, r0:r0 + c, sl[hd]] for bi, r0, hd in items]
        kbf = [k_ref[bi, r0:r0 + c, sl[hd]] for bi, r0, hd in items]
        vbf = [v_ref[bi, r0:r0 + c, sl[hd]] for bi, r0, hd in items]
        scaled = lambda xs, fs: [(x.astype(F32) * f).astype(BF16) for x, f in zip(xs, fs)]
        beta = [column(bgs[bi, r0], hd) for bi, r0, hd in items]
        eg = [column(egs[bi, r0], nh + hd) for bi, r0, hd in items]
        kb = scaled(kbf, beta)
        vb = scaled(vbf, beta)
        kbe = scaled(kbf, [b * e for b, e in zip(beta, eg)])
        qe = scaled(qbf, eg)
        kd = scaled(kbf, [column(eds[bi, r0], nh + hd) for bi, r0, hd in items])
        gcol = [gcums[bi, r0][:, nh + hd:nh + hd + 1] for bi, r0, hd in items]
        grow = [gcums_t[bi, r0][nh + hd:nh + hd + 1, :] for bi, r0, hd in items]
        decay = [jnp.where(tri, jnp.exp(jnp.where(tri, gc - gr, 0.0)), 0.0) for gc, gr in zip(gcol, grow)]
        kk = [lax.dot_general(a, b, NT_DIMS, preferred_element_type=F32) for a, b in zip(kb, kbf)]
        attn = [(lax.dot_general(a, b, NT_DIMS, preferred_element_type=F32) * d).astype(BF16)
                for a, b, d in zip(qbf, kbf, decay)]
        t_inv = [x.astype(BF16) for x in
                 _unit_lower_inverses([jnp.where(strict, a * d, 0.0) for a, d in zip(kk, decay)])]
        u = [_dot(ti, x) for ti, x in zip(t_inv, vb)]
        w = [_dot(ti, x).astype(BF16) for ti, x in zip(t_inv, kbe)]
        e_last = [egs[bi, r0][c - 1:c, nh + hd:nh + hd + 1] for bi, r0, hd in items]
        fields = (u, w, attn, qe, kd, e_last)
        return {ch: tuple(f[i * nh:(i + 1) * nh] for f in fields) for i, ch in enumerate(chunks)}

    def recur(r0, pres, state):
        u, w, attn, qe, kd, e_last = (sum((list(p[f]) for p in pres), []) for f in range(6))
        sb = [x.astype(BF16) for x in state]
        vnb = [(a - _dot(b, s)).astype(BF16) for a, b, s in zip(u, w, sb)]
        o = [_dot(x, s) + _dot(a, vn) for x, s, a, vn in zip(qe, sb, attn, vnb)]
        state = [s * e + lax.dot_general(x, vn, TN_DIMS, preferred_element_type=F32)
                 for s, e, x, vn in zip(state, e_last, kd, vnb)]
        for bi in range(nb):
            gated = [(_rms(o[bi * nh + hd], onorm)
                      * _silu(gate_ref[bi, r0:r0 + c, sl[hd]].astype(F32))).astype(BF16) for hd in range(nh)]
            out_ref[bi, r0:r0 + c, :] = jnp.concatenate(gated, axis=-1)
        return state

    starts = list(range(0, t, c))
    pre = intra([(bi, r0) for r0 in starts for bi in range(nb)])
    state = [state_ref[i] for i in range(nb * nh)]
    for r0 in starts:
        state = recur(r0, [pre[bi, r0] for bi in range(nb)], state)
    for i in range(nb * nh):
        state_ref[i] = state[i]


def _gdn_mixer(x, g, w_in_x, conv_w, a_log, dt_bias, o_norm):
    b, s, d = x.shape
    hk = GDN_HEADS * GDN_DK
    lanes_pad = jnp.zeros((LANES - 2 * GDN_HEADS,), F32)
    zeros_h = jnp.zeros((GDN_HEADS,), F32)
    alog_row = jnp.concatenate([zeros_h, a_log, lanes_pad])[None, :]
    dtb_row = jnp.concatenate([zeros_h, dt_bias, lanes_pad])[None, :]
    tm = _tile(s, TM_GDN_PROJ)
    row = lambda tt, c: pl.BlockSpec((None, tt, c), lambda bi, i: (bi, i, 0))
    vec = lambda c: pl.BlockSpec((1, c), lambda bi, i: (0, 0))
    q, k, v, gate, bg = pl.pallas_call(
        _gdn_proj_kernel,
        grid=(b, s // tm),
        in_specs=[row(tm, d), vec(d), _resident(w_in_x.shape),
                  pl.BlockSpec(conv_w.shape, lambda bi, i: (0, 0)), vec(LANES), vec(LANES)],
        out_specs=[row(tm, hk), row(tm, hk), row(tm, hk), row(tm, hk), row(tm, LANES)],
        out_shape=[jax.ShapeDtypeStruct((b, s, hk), BF16)] * 4 + [jax.ShapeDtypeStruct((b, s, LANES), F32)],
        scratch_shapes=[pltpu.VMEM((SUBLANES, 3 * hk), F32)],
        compiler_params=_params("parallel", "arbitrary"),
        name="gdn_proj",
    )(x, g[None, :], w_in_x, conv_w, alog_row, dtb_row)

    tt = _tile(s, T_GDN)
    nb = _tile(b, GDN_SEQS)
    seqs = lambda c: pl.BlockSpec((nb, tt, c), lambda bi, i: (bi, i, 0))
    return pl.pallas_call(
        _gdn_chunk_kernel,
        grid=(b // nb, s // tt),
        in_specs=[seqs(hk), seqs(hk), seqs(hk), seqs(hk), seqs(LANES), vec(GDN_DV)],
        out_specs=seqs(hk),
        out_shape=jax.ShapeDtypeStruct((b, s, hk), BF16),
        scratch_shapes=[pltpu.VMEM((nb * GDN_HEADS, GDN_DK, GDN_DV), F32)],
        compiler_params=_params("parallel", "arbitrary"),
        name="gdn_chunk",
    )(q, k, v, gate, bg, o_norm[None, :])


def kernel(x, mem, positions, mla_w_in, mla_q_norm, mla_kv_norm, mla_w_uq, mla_w_ukv, mla_w_o, gdn_w_in, gdn_conv_w, gdn_a_log, gdn_dt_bias, gdn_o_norm, gdn_w_o, sc_w_in, sc_conv_w, sc_w_o, norm_mix, norm_mem, norm_mlp, xa_w_q, xa_w_kv, xa_w_o, mlp_w1, mlp_w2, mem_norm, final_norm):
    b, s, d = x.shape
    depth = norm_mix.shape[0]
    m = mem.shape[1]
    cos_t, sin_t = _rope_tables(positions)
    kv_all = _mem_kv(mem.reshape(b * m, d), mem_norm, xa_w_kv).reshape(depth, b, m, 2 * d)
    for i in range(depth):
        j = i // N_MIXERS
        kind = i % N_MIXERS
        if kind == 0:
            w_in_x, w_uq_x, w_kn_t, w_v = _mla_weights(mla_w_in[j], mla_w_uq[j], mla_w_ukv[j])
            pre = _mla_mixer(x, norm_mix[i], w_in_x, mla_q_norm[j], mla_kv_norm[j], w_uq_x, w_kn_t, w_v,
                             cos_t, sin_t)
            w_mix = mla_w_o
        elif kind == 1:
            pad = jnp.zeros((d, LANES - 2 * GDN_HEADS), F32)
            w_in_x = jnp.concatenate([gdn_w_in[j], pad], axis=1).astype(BF16)
            pre = _gdn_mixer(x, norm_mix[i], w_in_x, gdn_conv_w[j], gdn_a_log[j], gdn_dt_bias[j],
                             gdn_o_norm[j])
            w_mix = gdn_w_o
        else:
            pre = _sc_mixer(x, norm_mix[i], sc_w_in, j, sc_conv_w[j])
            w_mix = sc_w_o
        x = _tail(x, pre, w_mix, j, norm_mem[i], xa_w_q, kv_all, i, xa_w_o)
        x = _mlp(x.reshape(b * s, d), norm_mlp[i], mlp_w1, mlp_w2, i,
                 final_norm, i == depth - 1).reshape(b, s, d)
    return x
```

```python
import functools

import jax
import jax.numpy as jnp
from jax import lax
from jax.experimental import pallas as pl
from jax.experimental.pallas import tpu as pltpu

F32 = jnp.float32
BF16 = jnp.bfloat16

EPS = 1e-6
ROPE_THETA = 10000.0
LOG2E = 1.4426950408889634
N_MIXERS = 3

MLA_HEADS = 8
MLA_NOPE = 128
MLA_ROPE = 64
MLA_V = 128
MLA_Q_RANK = 384
MLA_KV_RANK = 256

GDN_HEADS = 8
GDN_DK = 128
GDN_DV = 128
GDN_CONV = 4
GDN_CHUNK = 64

SC_CONV = 3
X_HEADS = 4

LANES = 128
SUBLANES = 8
VMEM_LIMIT_BYTES = 56 * 1024 * 1024

TM_TABLE = 1024
TM_MEM = 1024
TM_TAIL = 1024
TM_MLP = 512
TM_MLA_PROJ = 1024
TQ_MLA = 256
TK_MLA = 256
MLA_SCORE_LOOKAHEAD = 1
TM_SC = 1024
TM_GDN_PROJ = 256
T_GDN = 128
GDN_SEQS = 2
GDN_PROJ_COLS = 256
SC_COLS = 256
FF_CHUNK = 1024

NT_DIMS = (((1,), (1,)), ((), ()))
TN_DIMS = (((0,), (0,)), ((), ()))


def _tile(extent, tile):
    assert extent % tile == 0, (extent, tile)
    return tile


def _params(*sem):
    return pltpu.CompilerParams(dimension_semantics=sem, vmem_limit_bytes=VMEM_LIMIT_BYTES)


def _resident(shape):
    nd = len(shape)
    return pl.BlockSpec(shape, lambda *_: (0,) * nd, pipeline_mode=pl.Buffered(1))


def _layer_resident(shape, layer):
    nd = len(shape)
    return pl.BlockSpec((None,) + shape, lambda *_: (layer,) + (0,) * nd, pipeline_mode=pl.Buffered(1))


def _rms(x, g):
    return x * lax.rsqrt(jnp.mean(x * x, axis=-1, keepdims=True) + EPS) * g


def _dot(a, b):
    return jnp.dot(a, b, preferred_element_type=F32)


def _sigmoid(x):
    return 1.0 / (1.0 + jnp.exp2(x * -LOG2E))


def _silu(x):
    return x * _sigmoid(x)


def _rope_table_kernel(pos_ref, freq_ref, cos_ref, sin_ref):
    ang = pos_ref[...] * freq_ref[...]
    cos_ref[...] = jnp.cos(ang)
    sin_ref[...] = jnp.sin(ang)


def _rope_tables(positions):
    n = positions.size
    pos = jnp.broadcast_to(positions.reshape(n, 1).astype(F32), (n, LANES))
    inv_freq = ROPE_THETA ** (-jnp.arange(0, MLA_ROPE, 2, dtype=F32) / MLA_ROPE)
    freq = jnp.tile(inv_freq, 2 * LANES // MLA_ROPE)[None, :]
    tm = _tile(n, TM_TABLE)
    row = pl.BlockSpec((tm, LANES), lambda i: (i, 0))
    return pl.pallas_call(
        _rope_table_kernel,
        grid=(n // tm,),
        in_specs=[row, pl.BlockSpec((1, LANES), lambda i: (0, 0))],
        out_specs=[row, row],
        out_shape=[jax.ShapeDtypeStruct((n, LANES), F32)] * 2,
        compiler_params=_params("parallel"),
        name="rope_tables",
    )(pos, freq)


def _mem_kv_kernel(mem_ref, g_ref, w_ref, out_ref):
    mn = _rms(mem_ref[...], g_ref[...]).astype(BF16)
    out_ref[0] = _dot(mn, w_ref[0].astype(BF16)).astype(BF16)


def _mem_kv(mem2d, mem_norm, w_kv):
    n, d = mem2d.shape
    depth, _, d2 = w_kv.shape
    tm = _tile(n, min(TM_MEM, n))
    return pl.pallas_call(
        _mem_kv_kernel,
        grid=(depth, n // tm),
        in_specs=[
            pl.BlockSpec((tm, d), lambda l, i: (i, 0)),
            pl.BlockSpec((1, d), lambda l, i: (0, 0)),
            pl.BlockSpec((1, d, d2), lambda l, i: (l, 0, 0)),
        ],
        out_specs=pl.BlockSpec((1, tm, d2), lambda l, i: (l, i, 0)),
        out_shape=jax.ShapeDtypeStruct((depth, n, d2), BF16),
        compiler_params=_params("parallel", "parallel"),
        name="mem_kv",
    )(mem2d, mem_norm[None, :], w_kv)


def _tail_kernel(x_ref, pre_ref, wmix_ref, g_ref, wq_ref, kv_ref, wo_ref, out_ref):
    d = x_ref.shape[-1]
    dh = d // X_HEADS
    x = x_ref[...] + _dot(pre_ref[...], wmix_ref[...].astype(BF16))
    h = _rms(x, g_ref[...]).astype(BF16)
    q = (_dot(h, wq_ref[...].astype(BF16)) * (dh ** -0.5 * LOG2E)).astype(BF16)
    scores = [lax.dot_general(q[:, hd * dh:(hd + 1) * dh], kv_ref[:, hd * dh:(hd + 1) * dh], NT_DIMS,
                              preferred_element_type=F32) for hd in range(X_HEADS)]
    heads = []
    for hd, s in enumerate(scores):
        p = jnp.exp2(s - jnp.max(s, axis=-1, keepdims=True))
        l = jnp.sum(p, axis=-1, keepdims=True)
        heads.append((_dot(p.astype(BF16), kv_ref[:, d + hd * dh:d + (hd + 1) * dh]) / l).astype(BF16))
    o = jnp.concatenate(heads, axis=-1)
    out_ref[...] = x + _dot(o, wo_ref[...].astype(BF16))


def _tail(x, pre, w_mix, mix_layer, g_mem, w_q, kv_all, layer, w_o):
    b, s, d = x.shape
    m = kv_all.shape[2]
    tm = _tile(s, TM_TAIL)
    row = pl.BlockSpec((None, tm, d), lambda bi, i: (bi, i, 0))
    return pl.pallas_call(
        _tail_kernel,
        grid=(b, s // tm),
        in_specs=[
            row, row, _layer_resident((d, d), mix_layer),
            pl.BlockSpec((1, d), lambda bi, i: (0, 0)),
            _layer_resident((d, d), layer),
            pl.BlockSpec((None, None, m, 2 * d), lambda bi, i: (layer, bi, 0, 0)),
            _layer_resident((d, d), layer),
        ],
        out_specs=row,
        out_shape=jax.ShapeDtypeStruct((b, s, d), F32),
        compiler_params=_params("parallel", "parallel"),
        name="tail",
    )(x, pre, w_mix, g_mem[None, :], w_q, kv_all, w_o)


def _mlp_kernel(x_ref, g_ref, w1_ref, w2_ref, gf_ref, out_ref, *, final):
    x = x_ref[...]
    h = _rms(x, g_ref[...]).astype(BF16)
    acc = x
    up = lambda c: _dot(h, w1_ref[:, c:c + FF_CHUNK].astype(BF16))
    ff = w1_ref.shape[1]
    a_next = up(0)
    for c in range(0, ff, FF_CHUNK):
        a = jnp.maximum(a_next, 0.0)
        if c + FF_CHUNK < ff:
            a_next = up(c + FF_CHUNK)
        acc = acc + _dot((a * a).astype(BF16), w2_ref[c:c + FF_CHUNK, :].astype(BF16))
    if final:
        acc = _rms(acc, gf_ref[...])
    out_ref[...] = acc


def _mlp(x2d, g, w1, w2, layer, g_final, final):
    n, d = x2d.shape
    ff = w1.shape[2]
    tm = _tile(n, TM_MLP)
    row = pl.BlockSpec((tm, d), lambda i: (i, 0))
    vec = pl.BlockSpec((1, d), lambda i: (0, 0))
    return pl.pallas_call(
        functools.partial(_mlp_kernel, final=final),
        grid=(n // tm,),
        in_specs=[row, vec, _layer_resident((d, ff), layer), _layer_resident((ff, d), layer), vec],
        out_specs=row,
        out_shape=jax.ShapeDtypeStruct((n, d), F32),
        compiler_params=_params("parallel"),
        name="mlp",
    )(x2d, g[None, :], w1, w2, g_final[None, :])


def _mla_weights(w_in, w_uq, w_ukv):
    half = MLA_ROPE // 2

    def swap_cols(w):
        return jnp.concatenate([-w[..., half:], w[..., :half]], axis=-1)

    lat = MLA_Q_RANK + MLA_KV_RANK
    kr = w_in[:, lat:]
    w_in_x = jnp.concatenate([w_in[:, :lat], kr, kr, swap_cols(kr), swap_cols(kr)], axis=1)
    uq = w_uq.reshape(MLA_Q_RANK, MLA_HEADS, MLA_NOPE + MLA_ROPE)
    qn = uq[:, :, :MLA_NOPE].reshape(MLA_Q_RANK, -1)
    qr = uq[:, :, MLA_NOPE:]
    w_uq_x = jnp.concatenate(
        [qn, qr.reshape(MLA_Q_RANK, -1), swap_cols(qr).reshape(MLA_Q_RANK, -1)], axis=1)
    ukv = w_ukv.reshape(MLA_KV_RANK, MLA_HEADS, MLA_NOPE + MLA_V)
    w_kn_t = ukv[:, :, :MLA_NOPE].reshape(MLA_KV_RANK, -1).T
    w_v = ukv[:, :, MLA_NOPE:].reshape(MLA_KV_RANK, -1)
    return w_in_x.astype(BF16), w_uq_x.astype(BF16), w_kn_t.astype(BF16), w_v.astype(BF16)


def _mla_proj_kernel(x_ref, g_ref, win_ref, gq_ref, gkv_ref, wuq_ref, wknt_ref, wv_ref, cos_ref, sin_ref,
                     qn_ref, qr_ref, knt_ref, krt_ref, v_ref):
    hn = MLA_HEADS * MLA_NOPE
    hr = MLA_HEADS * MLA_ROPE
    lat = MLA_Q_RANK + MLA_KV_RANK
    scale = (MLA_NOPE + MLA_ROPE) ** -0.5 * LOG2E
    h = _rms(x_ref[...], g_ref[...]).astype(BF16)
    z = _dot(h, win_ref[...])
    cos = cos_ref[...]
    sin = sin_ref[...]
    kr_t = (z[:, lat:lat + LANES] * cos + z[:, lat + LANES:lat + 2 * LANES] * sin).T
    low = lax.broadcasted_iota(jnp.int32, kr_t.shape, 0) < MLA_ROPE
    krt_ref[0:LANES, :] = jnp.where(low, kr_t, 0.0).astype(BF16)
    krt_ref[LANES:2 * LANES, :] = jnp.where(low, 0.0, kr_t).astype(BF16)
    cq = _rms(z[:, :MLA_Q_RANK], gq_ref[...]).astype(BF16)
    ckv = _rms(z[:, MLA_Q_RANK:lat], gkv_ref[...])
    q = _dot(cq, wuq_ref[...]) * scale
    qn_ref[...] = q[:, :hn].astype(BF16)
    reps = hr // LANES
    cos_r = jnp.concatenate([cos] * reps, axis=-1)
    sin_r = jnp.concatenate([sin] * reps, axis=-1)
    qr_ref[...] = (q[:, hn:hn + hr] * cos_r + q[:, hn + hr:hn + 2 * hr] * sin_r).astype(BF16)
    knt_ref[...] = _dot(wknt_ref[...], ckv.T.astype(BF16)).astype(BF16)
    v_ref[...] = _dot(ckv.astype(BF16), wv_ref[...]).astype(BF16)


def _mla_attn_kernel(qn_ref, qr_ref, knt_ref, krt_ref, v_ref, o_ref):
    s_len = qn_ref.shape[0]
    tq = TQ_MLA
    tk = TK_MLA
    rows = lax.broadcasted_iota(jnp.int32, (tq, tk), 0)
    cols = lax.broadcasted_iota(jnp.int32, (tq, tk), 1)
    lanes = lambda hl: slice(hl * LANES, (hl + 1) * LANES)

    def scores(q0, hl):
        q1 = q0 + tq
        d0 = q0 // tk * tk
        q = jnp.concatenate([qn_ref[q0:q1, lanes(hl)], qr_ref[q0:q1, :]], axis=-1)
        k_diag = jnp.concatenate([knt_ref[lanes(hl), d0:d0 + tk], krt_ref[lanes(hl), d0:d0 + tk]], axis=0)
        s_diag = jnp.where(rows + (q0 - d0) >= cols, _dot(q, k_diag), -jnp.inf)
        if not d0:
            return s_diag, None
        k_past = jnp.concatenate([knt_ref[lanes(hl), 0:d0], krt_ref[lanes(hl), 0:d0]], axis=0)
        return s_diag, _dot(q, k_past)

    def softmax(unit, s_diag, s_past):
        m = jnp.max(s_diag, axis=-1, keepdims=True)
        if s_past is not None:
            m = jnp.maximum(m, jnp.max(s_past, axis=-1, keepdims=True))
        p_diag = jnp.exp2(s_diag - m)
        l = jnp.sum(p_diag, axis=-1, keepdims=True)
        p_past = None
        if s_past is not None:
            p_past = jnp.exp2(s_past - m)
            l = l + jnp.sum(p_past, axis=-1, keepdims=True)
            p_past = p_past.astype(BF16)
        return unit, p_diag.astype(BF16), p_past, l

    def values(unit, p_diag, p_past, l):
        q0, hl = unit
        d0 = q0 // tk * tk
        acc = _dot(p_diag, v_ref[d0:d0 + tk, :])
        if p_past is not None:
            acc = acc + _dot(p_past, v_ref[0:d0, :])
        o_ref[q0:q0 + tq, lanes(hl)] = (acc[:, lanes(hl)] / l).astype(BF16)

    units = [(q0, hl) for q0 in reversed(range(0, s_len, tq)) for hl in range(2)]
    ahead = MLA_SCORE_LOOKAHEAD
    queue = [scores(*u) for u in units[:ahead]]
    pending = None
    for idx, unit in enumerate(units):
        cur = queue.pop(0)
        if idx + ahead < len(units):
            queue.append(scores(*units[idx + ahead]))
        if pending is not None:
            values(*pending)
        pending = softmax(unit, *cur)
    values(*pending)


def _mla_mixer(x, g, w_in_x, gq, gkv, w_uq_x, w_kn_t, w_v, cos_t, sin_t):
    b, s, d = x.shape
    hn = MLA_HEADS * MLA_NOPE
    hr = MLA_HEADS * MLA_ROPE
    tm = _tile(s, TM_MLA_PROJ)
    _tile(s, TK_MLA)
    row = lambda w: pl.BlockSpec((None, tm, w), lambda bi, i: (bi, i, 0))
    col = lambda r: pl.BlockSpec((None, r, tm), lambda bi, i: (bi, 0, i))
    vec = lambda w: pl.BlockSpec((1, w), lambda bi, i: (0, 0))
    qn, qr, knt, krt, v = pl.pallas_call(
        _mla_proj_kernel,
        grid=(b, s // tm),
        in_specs=[row(d), vec(d), _resident(w_in_x.shape), vec(MLA_Q_RANK), vec(MLA_KV_RANK),
                  _resident(w_uq_x.shape), _resident(w_kn_t.shape), _resident(w_v.shape),
                  row(LANES), row(LANES)],
        out_specs=[row(hn), row(hr), col(hn), col(2 * LANES), row(hn)],
        out_shape=[jax.ShapeDtypeStruct((b, s, hn), BF16), jax.ShapeDtypeStruct((b, s, hr), BF16),
                   jax.ShapeDtypeStruct((b, hn, s), BF16), jax.ShapeDtypeStruct((b, 2 * LANES, s), BF16),
                   jax.ShapeDtypeStruct((b, s, hn), BF16)],
        compiler_params=_params("parallel", "parallel"),
        name="mla_proj",
    )(x, g[None, :], w_in_x, gq[None, :], gkv[None, :], w_uq_x, w_kn_t, w_v,
      cos_t.reshape(b, s, LANES), sin_t.reshape(b, s, LANES))

    pair = pl.BlockSpec((None, s, 2 * LANES), lambda bi, hp: (bi, 0, hp))
    return pl.pallas_call(
        _mla_attn_kernel,
        grid=(b, MLA_HEADS // 2),
        in_specs=[pair,
                  pl.BlockSpec((None, s, LANES), lambda bi, hp: (bi, 0, hp)),
                  pl.BlockSpec((None, 2 * LANES, s), lambda bi, hp: (bi, hp, 0)),
                  pl.BlockSpec((None, 2 * LANES, s), lambda bi, hp: (bi, 0, 0)),
                  pair],
        out_specs=pair,
        out_shape=jax.ShapeDtypeStruct((b, s, hn), BF16),
        compiler_params=_params("parallel", "parallel"),
        name="mla_attn",
    )(qn, qr, knt, krt, v)


def _zero_at_sequence_start(carry_ref):
    @pl.when(pl.program_id(1) == 0)
    def _():
        carry_ref[...] = jnp.zeros(carry_ref.shape, F32)


def _causal_conv(z, w_ref, col, carry_ref):
    tm, c = z.shape
    k = w_ref.shape[0]
    head = jnp.concatenate([carry_ref[...], z[0:SUBLANES, :]], axis=0)
    y = w_ref[k - 1:k, col:col + c] * z
    y_head = y[0:SUBLANES, :]
    for j in range(1, k):
        tap = w_ref[k - 1 - j:k - j, col:col + c]
        y = y + tap * pltpu.roll(z, j, 0)
        y_head = y_head + tap * pltpu.roll(head, j, 0)[SUBLANES:, :]
    carry_ref[...] = z[tm - SUBLANES:, :]
    return jnp.concatenate([y_head, y[SUBLANES:, :]], axis=0)


def _sc_kernel(x_ref, g_ref, win_ref, cw_ref, out_ref, buf_ref):
    w = out_ref.shape[-1]
    _zero_at_sequence_start(buf_ref)
    h = _rms(x_ref[...], g_ref[...]).astype(BF16)
    cb = SC_COLS

    def projections(c0):
        return [_dot(h, win_ref[:, part * w + c0:part * w + c0 + cb].astype(BF16)) for part in range(3)]

    nxt = projections(0)
    for c0 in range(0, w, cb):
        zb, zc, zu = nxt
        if c0 + cb < w:
            nxt = projections(c0 + cb)
        y = _causal_conv(zc * zu, cw_ref, c0, buf_ref.at[:, c0:c0 + cb])
        out_ref[:, c0:c0 + cb] = (zb * y).astype(BF16)


def _sc_mixer(x, g, w_in, layer, conv_w):
    b, s, d = x.shape
    w = conv_w.shape[1]
    tm = _tile(s, TM_SC)
    row = lambda c: pl.BlockSpec((None, tm, c), lambda bi, i: (bi, i, 0))
    return pl.pallas_call(
        _sc_kernel,
        grid=(b, s // tm),
        in_specs=[row(d), pl.BlockSpec((1, d), lambda bi, i: (0, 0)),
                  _layer_resident(w_in.shape[1:], layer),
                  pl.BlockSpec(conv_w.shape, lambda bi, i: (0, 0))],
        out_specs=row(w),
        out_shape=jax.ShapeDtypeStruct((b, s, w), BF16),
        scratch_shapes=[pltpu.VMEM((SUBLANES, w), F32)],
        compiler_params=_params("parallel", "arbitrary"),
        name="short_conv",
    )(x, g[None, :], w_in, conv_w)


def _gdn_proj_kernel(x_ref, g_ref, win_ref, cw_ref, alog_ref, dtb_ref,
                     q_ref, k_ref, v_ref, gate_ref, bg_ref, buf_ref):
    hk = GDN_HEADS * GDN_DK
    _zero_at_sequence_start(buf_ref)
    h = _rms(x_ref[...], g_ref[...]).astype(BF16)
    cb = GDN_PROJ_COLS
    proj = lambda c0: _dot(h, win_ref[:, c0:c0 + cb])
    gate_cols = list(range(0, hk, cb))
    z_next = proj(0)
    for c0 in range(0, 3 * hk, cb):
        z = z_next
        if c0 + cb < 3 * hk:
            z_next = proj(c0 + cb)
        if gate_cols:
            g0 = gate_cols.pop(0)
            gate_ref[:, g0:g0 + cb] = proj(3 * hk + g0).astype(BF16)
        part, off = divmod(c0, hk)
        ref = (q_ref, k_ref, v_ref)[part]
        y = _silu(_causal_conv(z, cw_ref, c0, buf_ref.at[:, c0:c0 + cb]))
        if part < 2:
            post = GDN_DK ** -0.5 if part == 0 else 1.0
            for d0 in range(0, cb, GDN_DK):
                yh = y[:, d0:d0 + GDN_DK]
                inv = lax.rsqrt(jnp.sum(yh * yh, axis=-1, keepdims=True) + EPS) * post
                ref[:, off + d0:off + d0 + GDN_DK] = (yh * inv).astype(BF16)
        else:
            ref[:, off:off + cb] = y.astype(BF16)
    for g0 in gate_cols:
        gate_ref[:, g0:g0 + cb] = proj(3 * hk + g0).astype(BF16)
    zb = _dot(h, win_ref[:, 4 * hk:4 * hk + LANES])
    beta = jax.nn.sigmoid(zb)
    t = zb + dtb_ref[...]
    softplus = jnp.maximum(t, 0.0) + jnp.log1p(jnp.exp(-jnp.abs(t)))
    decay = -jnp.exp(alog_ref[...]) * softplus
    lane = lax.broadcasted_iota(jnp.int32, zb.shape, 1)
    bg_ref[...] = jnp.where(lane < GDN_HEADS, beta, decay)


def _cumsum_rows(x):
    n = x.shape[0]
    row = lax.broadcasted_iota(jnp.int32, x.shape, 0)
    shift = 1
    while shift < n:
        x = x + jnp.where(row >= shift, pltpu.roll(x, shift, 0), 0.0)
        shift *= 2
    return x


def _unit_lower_inverses(ms):
    c = ms[0].shape[0]
    eye = (lax.broadcasted_iota(jnp.int32, (c, c), 0) == lax.broadcasted_iota(jnp.int32, (c, c), 1)).astype(F32)
    invs = [eye - m for m in ms]
    powers = [m.astype(BF16) for m in ms]
    width = 2
    while width < c:
        powers = [_dot(p, p).astype(BF16) for p in powers]
        invs = [inv + _dot(inv.astype(BF16), p) for inv, p in zip(invs, powers)]
        width *= 2
    return invs


def _gdn_chunk_kernel(q_ref, k_ref, v_ref, gate_ref, bg_ref, onorm_ref, out_ref, state_ref):
    c = GDN_CHUNK
    nb, t = q_ref.shape[0], q_ref.shape[1]
    nh = GDN_HEADS

    @pl.when(pl.program_id(1) == 0)
    def _():
        state_ref[...] = jnp.zeros(state_ref.shape, F32)

    rows = lax.broadcasted_iota(jnp.int32, (c, c), 0)
    cols = lax.broadcasted_iota(jnp.int32, (c, c), 1)
    tri = rows >= cols
    strict = rows > cols
    onorm = onorm_ref[...]
    sl = [slice(hd * GDN_DK, (hd + 1) * GDN_DK) for hd in range(nh)]

    def intra(chunks):
        items = [(bi, r0, hd) for bi, r0 in chunks for hd in range(nh)]
        bgs = {ch: bg_ref[ch[0], ch[1]:ch[1] + c, :] for ch in chunks}
        gcums = {ch: _cumsum_rows(bgs[ch]) for ch in chunks}
        gcums_t = {ch: gcums[ch].T for ch in chunks}
        egs = {ch: jnp.exp(gcums[ch]) for ch in chunks}
        eds = {ch: jnp.exp(gcums[ch][c - 1:c, :] - gcums[ch]) for ch in chunks}
        column = lambda a, lane: jnp.broadcast_to(a[:, lane:lane + 1], (c, GDN_DK))
        qbf = [q_ref[bi, r0:r0 + c, sl[hd]] for bi, r0, hd in items]
        kbf = [k_ref[bi, r0:r0 + c, sl[hd]] for bi, r0, hd in items]
        vbf = [v_ref[bi, r0:r0 + c, sl[hd]] for bi, r0, hd in items]
        scaled = lambda xs, fs: [(x.astype(F32) * f).astype(BF16) for x, f in zip(xs, fs)]
        beta = [column(bgs[bi, r0], hd) for bi, r0, hd in items]
        eg = [column(egs[bi, r0], nh + hd) for bi, r0, hd in items]
        kb = scaled(kbf, beta)
        vb = scaled(vbf, beta)
        kbe = scaled(kbf, [b * e for b, e in zip(beta, eg)])
        qe = scaled(qbf, eg)
        kd = scaled(kbf, [column(eds[bi, r0], nh + hd) for bi, r0, hd in items])
        gcol = [gcums[bi, r0][:, nh + hd:nh + hd + 1] for bi, r0, hd in items]
        grow = [gcums_t[bi, r0][nh + hd:nh + hd + 1, :] for bi, r0, hd in items]
        decay = [jnp.where(tri, jnp.exp(jnp.where(tri, gc - gr, 0.0)), 0.0) for gc, gr in zip(gcol, grow)]
        kk = [lax.dot_general(a, b, NT_DIMS, preferred_element_type=F32) for a, b in zip(kb, kbf)]
        attn = [(lax.dot_general(a, b, NT_DIMS, preferred_element_type=F32) * d).astype(BF16)
                for a, b, d in zip(qbf, kbf, decay)]
        t_inv = [x.astype(BF16) for x in
                 _unit_lower_inverses([jnp.where(strict, a * d, 0.0) for a, d in zip(kk, decay)])]
        u = [_dot(ti, x) for ti, x in zip(t_inv, vb)]
        w = [_dot(ti, x).astype(BF16) for ti, x in zip(t_inv, kbe)]
        e_last = [egs[bi, r0][c - 1:c, nh + hd:nh + hd + 1] for bi, r0, hd in items]
        fields = (u, w, attn, qe, kd, e_last)
        return {ch: tuple(f[i * nh:(i + 1) * nh] for f in fields) for i, ch in enumerate(chunks)}

    def recur(r0, pres, state):
        u, w, attn, qe, kd, e_last = (sum((list(p[f]) for p in pres), []) for f in range(6))
        sb = [x.astype(BF16) for x in state]
        vnb = [(a - _dot(b, s)).astype(BF16) for a, b, s in zip(u, w, sb)]
        o = [_dot(x, s) + _dot(a, vn) for x, s, a, vn in zip(qe, sb, attn, vnb)]
        state = [s * e + lax.dot_general(x, vn, TN_DIMS, preferred_element_type=F32)
                 for s, e, x, vn in zip(state, e_last, kd, vnb)]
        for bi in range(nb):
            gated = [(_rms(o[bi * nh + hd], onorm)
                      * _silu(gate_ref[bi, r0:r0 + c, sl[hd]].astype(F32))).astype(BF16) for hd in range(nh)]
            out_ref[bi, r0:r0 + c, :] = jnp.concatenate(gated, axis=-1)
        return state

    starts = list(range(0, t, c))
    pre = intra([(bi, r0) for r0 in starts for bi in range(nb)])
    state = [state_ref[i] for i in range(nb * nh)]
    for r0 in starts:
        state = recur(r0, [pre[bi, r0] for bi in range(nb)], state)
    for i in range(nb * nh):
        state_ref[i] = state[i]


def _gdn_mixer(x, g, w_in_x, conv_w, a_log, dt_bias, o_norm):
    b, s, d = x.shape
    hk = GDN_HEADS * GDN_DK
    lanes_pad = jnp.zeros((LANES - 2 * GDN_HEADS,), F32)
    zeros_h = jnp.zeros((GDN_HEADS,), F32)
    alog_row = jnp.concatenate([zeros_h, a_log, lanes_pad])[None, :]
    dtb_row = jnp.concatenate([zeros_h, dt_bias, lanes_pad])[None, :]
    tm = _tile(s, TM_GDN_PROJ)
    row = lambda tt, c: pl.BlockSpec((None, tt, c), lambda bi, i: (bi, i, 0))
    vec = lambda c: pl.BlockSpec((1, c), lambda bi, i: (0, 0))
    q, k, v, gate, bg = pl.pallas_call(
        _gdn_proj_kernel,
        grid=(b, s // tm),
        in_specs=[row(tm, d), vec(d), _resident(w_in_x.shape),
                  pl.BlockSpec(conv_w.shape, lambda bi, i: (0, 0)), vec(LANES), vec(LANES)],
        out_specs=[row(tm, hk), row(tm, hk), row(tm, hk), row(tm, hk), row(tm, LANES)],
        out_shape=[jax.ShapeDtypeStruct((b, s, hk), BF16)] * 4 + [jax.ShapeDtypeStruct((b, s, LANES), F32)],
        scratch_shapes=[pltpu.VMEM((SUBLANES, 3 * hk), F32)],
        compiler_params=_params("parallel", "arbitrary"),
        name="gdn_proj",
    )(x, g[None, :], w_in_x, conv_w, alog_row, dtb_row)

    tt = _tile(s, T_GDN)
    nb = _tile(b, GDN_SEQS)
    seqs = lambda c: pl.BlockSpec((nb, tt, c), lambda bi, i: (bi, i, 0))
    return pl.pallas_call(
        _gdn_chunk_kernel,
        grid=(b // nb, s // tt),
        in_specs=[seqs(hk), seqs(hk), seqs(hk), seqs(hk), seqs(LANES), vec(GDN_DV)],
        out_specs=seqs(hk),
        out_shape=jax.ShapeDtypeStruct((b, s, hk), BF16),
        scratch_shapes=[pltpu.VMEM((nb * GDN_HEADS, GDN_DK, GDN_DV), F32)],
        compiler_params=_params("parallel", "arbitrary"),
        name="gdn_chunk",
    )(q, k, v, gate, bg, o_norm[None, :])


def kernel(x, mem, positions, mla_w_in, mla_q_norm, mla_kv_norm, mla_w_uq, mla_w_ukv, mla_w_o, gdn_w_in, gdn_conv_w, gdn_a_log, gdn_dt_bias, gdn_o_norm, gdn_w_o, sc_w_in, sc_conv_w, sc_w_o, norm_mix, norm_mem, norm_mlp, xa_w_q, xa_w_kv, xa_w_o, mlp_w1, mlp_w2, mem_norm, final_norm):
    b, s, d = x.shape
    depth = norm_mix.shape[0]
    m = mem.shape[1]
    cos_t, sin_t = _rope_tables(positions)
    kv_all = _mem_kv(mem.reshape(b * m, d), mem_norm, xa_w_kv).reshape(depth, b, m, 2 * d)
    for i in range(depth):
        j = i // N_MIXERS
        kind = i % N_MIXERS
        if kind == 0:
            w_in_x, w_uq_x, w_kn_t, w_v = _mla_weights(mla_w_in[j], mla_w_uq[j], mla_w_ukv[j])
            pre = _mla_mixer(x, norm_mix[i], w_in_x, mla_q_norm[j], mla_kv_norm[j], w_uq_x, w_kn_t, w_v,
                             cos_t, sin_t)
            w_mix = mla_w_o
        elif kind == 1:
            pad = jnp.zeros((d, LANES - 2 * GDN_HEADS), F32)
            w_in_x = jnp.concatenate([gdn_w_in[j], pad], axis=1).astype(BF16)
            pre = _gdn_mixer(x, norm_mix[i], w_in_x, gdn_conv_w[j], gdn_a_log[j], gdn_dt_bias[j],
                             gdn_o_norm[j])
            w_mix = gdn_w_o
        else:
            pre = _sc_mixer(x, norm_mix[i], sc_w_in, j, sc_conv_w[j])
            w_mix = sc_w_o
        x = _tail(x, pre, w_mix, j, norm_mem[i], xa_w_q, kv_all, i, xa_w_o)
        x = _mlp(x.reshape(b * s, d), norm_mlp[i], mlp_w1, mlp_w2, i,
                 final_norm, i == depth - 1).reshape(b, s, d)
    return x
```

```python
import functools

import jax
import jax.numpy as jnp
from jax import lax
from jax.experimental import pallas as pl
from jax.experimental.pallas import tpu as pltpu

F32 = jnp.float32
BF16 = jnp.bfloat16

EPS = 1e-6
ROPE_THETA = 10000.0
LOG2E = 1.4426950408889634
N_MIXERS = 3

MLA_HEADS = 8
MLA_NOPE = 128
MLA_ROPE = 64
MLA_V = 128
MLA_Q_RANK = 384
MLA_KV_RANK = 256

GDN_HEADS = 8
GDN_DK = 128
GDN_DV = 128
GDN_CONV = 4
GDN_CHUNK = 64

SC_CONV = 3
X_HEADS = 4

LANES = 128
SUBLANES = 8
VMEM_LIMIT_BYTES = 56 * 1024 * 1024

TM_TABLE = 1024
TM_MEM = 1024
TM_TAIL = 1024
TM_MLP = 512
TM_MLA_PROJ = 1024
TQ_MLA = 256
TK_MLA = 256
MLA_SCORE_LOOKAHEAD = 1
TM_SC = 1024
TM_GDN_PROJ = 256
T_GDN = 128
GDN_SEQS = 2
GDN_PROJ_COLS = 256
SC_COLS = 256
FF_CHUNK = 1024

NT_DIMS = (((1,), (1,)), ((), ()))
TN_DIMS = (((0,), (0,)), ((), ()))


def _tile(extent, tile):
    assert extent % tile == 0, (extent, tile)
    return tile


def _params(*sem):
    return pltpu.CompilerParams(dimension_semantics=sem, vmem_limit_bytes=VMEM_LIMIT_BYTES)


def _resident(shape):
    nd = len(shape)
    return pl.BlockSpec(shape, lambda *_: (0,) * nd, pipeline_mode=pl.Buffered(1))


def _layer_resident(shape, layer):
    nd = len(shape)
    return pl.BlockSpec((None,) + shape, lambda *_: (layer,) + (0,) * nd, pipeline_mode=pl.Buffered(1))


def _rms(x, g):
    return x * lax.rsqrt(jnp.mean(x * x, axis=-1, keepdims=True) + EPS) * g


def _dot(a, b):
    return jnp.dot(a, b, preferred_element_type=F32)


def _sigmoid(x):
    return 1.0 / (1.0 + jnp.exp2(x * -LOG2E))


def _silu(x):
    return x * _sigmoid(x)


def _rope_table_kernel(pos_ref, freq_ref, cos_ref, sin_ref):
    ang = pos_ref[...] * freq_ref[...]
    cos_ref[...] = jnp.cos(ang)
    sin_ref[...] = jnp.sin(ang)


def _rope_tables(positions):
    n = positions.size
    pos = jnp.broadcast_to(positions.reshape(n, 1).astype(F32), (n, LANES))
    inv_freq = ROPE_THETA ** (-jnp.arange(0, MLA_ROPE, 2, dtype=F32) / MLA_ROPE)
    freq = jnp.tile(inv_freq, 2 * LANES // MLA_ROPE)[None, :]
    tm = _tile(n, TM_TABLE)
    row = pl.BlockSpec((tm, LANES), lambda i: (i, 0))
    return pl.pallas_call(
        _rope_table_kernel,
        grid=(n // tm,),
        in_specs=[row, pl.BlockSpec((1, LANES), lambda i: (0, 0))],
        out_specs=[row, row],
        out_shape=[jax.ShapeDtypeStruct((n, LANES), F32)] * 2,
        compiler_params=_params("parallel"),
        name="rope_tables",
    )(pos, freq)


def _mem_kv_kernel(mem_ref, g_ref, w_ref, out_ref):
    mn = _rms(mem_ref[...], g_ref[...]).astype(BF16)
    out_ref[0] = _dot(mn, w_ref[0].astype(BF16)).astype(BF16)


def _mem_kv(mem2d, mem_norm, w_kv):
    n, d = mem2d.shape
    depth, _, d2 = w_kv.shape
    tm = _tile(n, min(TM_MEM, n))
    return pl.pallas_call(
        _mem_kv_kernel,
        grid=(depth, n // tm),
        in_specs=[
            pl.BlockSpec((tm, d), lambda l, i: (i, 0)),
            pl.BlockSpec((1, d), lambda l, i: (0, 0)),
            pl.BlockSpec((1, d, d2), lambda l, i: (l, 0, 0)),
        ],
        out_specs=pl.BlockSpec((1, tm, d2), lambda l, i: (l, i, 0)),
        out_shape=jax.ShapeDtypeStruct((depth, n, d2), BF16),
        compiler_params=_params("parallel", "parallel"),
        name="mem_kv",
    )(mem2d, mem_norm[None, :], w_kv)


def _tail_kernel(x_ref, pre_ref, wmix_ref, g_ref, wq_ref, kv_ref, wo_ref, out_ref):
    d = x_ref.shape[-1]
    dh = d // X_HEADS
    x = x_ref[...] + _dot(pre_ref[...], wmix_ref[...].astype(BF16))
    h = _rms(x, g_ref[...]).astype(BF16)
    q = (_dot(h, wq_ref[...].astype(BF16)) * (dh ** -0.5 * LOG2E)).astype(BF16)
    scores = [lax.dot_general(q[:, hd * dh:(hd + 1) * dh], kv_ref[:, hd * dh:(hd + 1) * dh], NT_DIMS,
                              preferred_element_type=F32) for hd in range(X_HEADS)]
    heads = []
    for hd, s in enumerate(scores):
        p = jnp.exp2(s - jnp.max(s, axis=-1, keepdims=True))
        l = jnp.sum(p, axis=-1, keepdims=True)
        heads.append((_dot(p.astype(BF16), kv_ref[:, d + hd * dh:d + (hd + 1) * dh]) / l).astype(BF16))
    o = jnp.concatenate(heads, axis=-1)
    out_ref[...] = x + _dot(o, wo_ref[...].astype(BF16))


def _tail(x, pre, w_mix, mix_layer, g_mem, w_q, kv_all, layer, w_o):
    b, s, d = x.shape
    m = kv_all.shape[2]
    tm = _tile(s, TM_TAIL)
    row = pl.BlockSpec((None, tm, d), lambda bi, i: (bi, i, 0))
    return pl.pallas_call(
        _tail_kernel,
        grid=(b, s // tm),
        in_specs=[
            row, row, _layer_resident((d, d), mix_layer),
            pl.BlockSpec((1, d), lambda bi, i: (0, 0)),
            _layer_resident((d, d), layer),
            pl.BlockSpec((None, None, m, 2 * d), lambda bi, i: (layer, bi, 0, 0)),
            _layer_resident((d, d), layer),
        ],
        out_specs=row,
        out_shape=jax.ShapeDtypeStruct((b, s, d), F32),
        compiler_params=_params("parallel", "parallel"),
        name="tail",
    )(x, pre, w_mix, g_mem[None, :], w_q, kv_all, w_o)


def _mlp_kernel(x_ref, g_ref, w1_ref, w2_ref, gf_ref, out_ref, *, final):
    x = x_ref[...]
    h = _rms(x, g_ref[...]).astype(BF16)
    acc = x
    up = lambda c: _dot(h, w1_ref[:, c:c + FF_CHUNK].astype(BF16))
    ff = w1_ref.shape[1]
    a_next = up(0)
    for c in range(0, ff, FF_CHUNK):
        a = jnp.maximum(a_next, 0.0)
        if c + FF_CHUNK < ff:
            a_next = up(c + FF_CHUNK)
        acc = acc + _dot((a * a).astype(BF16), w2_ref[c:c + FF_CHUNK, :].astype(BF16))
    if final:
        acc = _rms(acc, gf_ref[...])
    out_ref[...] = acc


def _mlp(x2d, g, w1, w2, layer, g_final, final):
    n, d = x2d.shape
    ff = w1.shape[2]
    tm = _tile(n, TM_MLP)
    row = pl.BlockSpec((tm, d), lambda i: (i, 0))
    vec = pl.BlockSpec((1, d), lambda i: (0, 0))
    return pl.pallas_call(
        functools.partial(_mlp_kernel, final=final),
        grid=(n // tm,),
        in_specs=[row, vec, _layer_resident((d, ff), layer), _layer_resident((ff, d), layer), vec],
        out_specs=row,
        out_shape=jax.ShapeDtypeStruct((n, d), F32),
        compiler_params=_params("parallel"),
        name="mlp",
    )(x2d, g[None, :], w1, w2, g_final[None, :])


def _mla_weights(w_in, w_uq, w_ukv):
    half = MLA_ROPE // 2

    def swap_cols(w):
        return jnp.concatenate([-w[..., half:], w[..., :half]], axis=-1)

    lat = MLA_Q_RANK + MLA_KV_RANK
    kr = w_in[:, lat:]
    w_in_x = jnp.concatenate([w_in[:, :lat], kr, kr, swap_cols(kr), swap_cols(kr)], axis=1)
    uq = w_uq.reshape(MLA_Q_RANK, MLA_HEADS, MLA_NOPE + MLA_ROPE)
    qn = uq[:, :, :MLA_NOPE].reshape(MLA_Q_RANK, -1)
    qr = uq[:, :, MLA_NOPE:]
    w_uq_x = jnp.concatenate(
        [qn, qr.reshape(MLA_Q_RANK, -1), swap_cols(qr).reshape(MLA_Q_RANK, -1)], axis=1)
    ukv = w_ukv.reshape(MLA_KV_RANK, MLA_HEADS, MLA_NOPE + MLA_V)
    w_kn_t = ukv[:, :, :MLA_NOPE].reshape(MLA_KV_RANK, -1).T
    w_v = ukv[:, :, MLA_NOPE:].reshape(MLA_KV_RANK, -1)
    return w_in_x.astype(BF16), w_uq_x.astype(BF16), w_kn_t.astype(BF16), w_v.astype(BF16)


def _mla_proj_kernel(x_ref, g_ref, win_ref, gq_ref, gkv_ref, wuq_ref, wknt_ref, wv_ref, cos_ref, sin_ref,
                     qn_ref, qr_ref, knt_ref, krt_ref, v_ref):
    hn = MLA_HEADS * MLA_NOPE
    hr = MLA_HEADS * MLA_ROPE
    lat = MLA_Q_RANK + MLA_KV_RANK
    scale = (MLA_NOPE + MLA_ROPE) ** -0.5 * LOG2E
    h = _rms(x_ref[...], g_ref[...]).astype(BF16)
    z = _dot(h, win_ref[...])
    cos = cos_ref[...]
    sin = sin_ref[...]
    kr_t = (z[:, lat:lat + LANES] * cos + z[:, lat + LANES:lat + 2 * LANES] * sin).T
    low = lax.broadcasted_iota(jnp.int32, kr_t.shape, 0) < MLA_ROPE
    krt_ref[0:LANES, :] = jnp.where(low, kr_t, 0.0).astype(BF16)
    krt_ref[LANES:2 * LANES, :] = jnp.where(low, 0.0, kr_t).astype(BF16)
    cq = _rms(z[:, :MLA_Q_RANK], gq_ref[...]).astype(BF16)
    ckv = _rms(z[:, MLA_Q_RANK:lat], gkv_ref[...])
    q = _dot(cq, wuq_ref[...]) * scale
    qn_ref[...] = q[:, :hn].astype(BF16)
    reps = hr // LANES
    cos_r = jnp.concatenate([cos] * reps, axis=-1)
    sin_r = jnp.concatenate([sin] * reps, axis=-1)
    qr_ref[...] = (q[:, hn:hn + hr] * cos_r + q[:, hn + hr:hn + 2 * hr] * sin_r).astype(BF16)
    knt_ref[...] = _dot(wknt_ref[...], ckv.T.astype(BF16)).astype(BF16)
    v_ref[...] = _dot(ckv.astype(BF16), wv_ref[...]).astype(BF16)


def _mla_attn_kernel(qn_ref, qr_ref, knt_ref, krt_ref, v_ref, o_ref):
    s_len = qn_ref.shape[0]
    tq = TQ_MLA
    tk = TK_MLA
    rows = lax.broadcasted_iota(jnp.int32, (tq, tk), 0)
    cols = lax.broadcasted_iota(jnp.int32, (tq, tk), 1)
    lanes = lambda hl: slice(hl * LANES, (hl + 1) * LANES)

    def scores(q0, hl):
        q1 = q0 + tq
        d0 = q0 // tk * tk
        q = jnp.concatenate([qn_ref[q0:q1, lanes(hl)], qr_ref[q0:q1, :]], axis=-1)
        k_diag = jnp.concatenate([knt_ref[lanes(hl), d0:d0 + tk], krt_ref[lanes(hl), d0:d0 + tk]], axis=0)
        s_diag = jnp.where(rows + (q0 - d0) >= cols, _dot(q, k_diag), -jnp.inf)
        if not d0:
            return s_diag, None
        k_past = jnp.concatenate([knt_ref[lanes(hl), 0:d0], krt_ref[lanes(hl), 0:d0]], axis=0)
        return s_diag, _dot(q, k_past)

    def softmax(unit, s_diag, s_past):
        m = jnp.max(s_diag, axis=-1, keepdims=True)
        if s_past is not None:
            m = jnp.maximum(m, jnp.max(s_past, axis=-1, keepdims=True))
        p_diag = jnp.exp2(s_diag - m)
        l = jnp.sum(p_diag, axis=-1, keepdims=True)
        p_past = None
        if s_past is not None:
            p_past = jnp.exp2(s_past - m)
            l = l + jnp.sum(p_past, axis=-1, keepdims=True)
            p_past = p_past.astype(BF16)
        return unit, p_diag.astype(BF16), p_past, l

    def values(unit, p_diag, p_past, l):
        q0, hl = unit
        d0 = q0 // tk * tk
        acc = _dot(p_diag, v_ref[d0:d0 + tk, :])
        if p_past is not None:
            acc = acc + _dot(p_past, v_ref[0:d0, :])
        o_ref[q0:q0 + tq, lanes(hl)] = (acc[:, lanes(hl)] / l).astype(BF16)

    units = [(q0, hl) for q0 in reversed(range(0, s_len, tq)) for hl in range(2)]
    ahead = MLA_SCORE_LOOKAHEAD
    queue = [scores(*u) for u in units[:ahead]]
    pending = None
    for idx, unit in enumerate(units):
        cur = queue.pop(0)
        if idx + ahead < len(units):
            queue.append(scores(*units[idx + ahead]))
        if pending is not None:
            values(*pending)
        pending = softmax(unit, *cur)
    values(*pending)


def _mla_mixer(x, g, w_in_x, gq, gkv, w_uq_x, w_kn_t, w_v, cos_t, sin_t):
    b, s, d = x.shape
    hn = MLA_HEADS * MLA_NOPE
    hr = MLA_HEADS * MLA_ROPE
    tm = _tile(s, TM_MLA_PROJ)
    _tile(s, TK_MLA)
    row = lambda w: pl.BlockSpec((None, tm, w), lambda bi, i: (bi, i, 0))
    col = lambda r: pl.BlockSpec((None, r, tm), lambda bi, i: (bi, 0, i))
    vec = lambda w: pl.BlockSpec((1, w), lambda bi, i: (0, 0))
    qn, qr, knt, krt, v = pl.pallas_call(
        _mla_proj_kernel,
        grid=(b, s // tm),
        in_specs=[row(d), vec(d), _resident(w_in_x.shape), vec(MLA_Q_RANK), vec(MLA_KV_RANK),
                  _resident(w_uq_x.shape), _resident(w_kn_t.shape), _resident(w_v.shape),
                  row(LANES), row(LANES)],
        out_specs=[row(hn), row(hr), col(hn), col(2 * LANES), row(hn)],
        out_shape=[jax.ShapeDtypeStruct((b, s, hn), BF16), jax.ShapeDtypeStruct((b, s, hr), BF16),
                   jax.ShapeDtypeStruct((b, hn, s), BF16), jax.ShapeDtypeStruct((b, 2 * LANES, s), BF16),
                   jax.ShapeDtypeStruct((b, s, hn), BF16)],
        compiler_params=_params("parallel", "parallel"),
        name="mla_proj",
    )(x, g[None, :], w_in_x, gq[None, :], gkv[None, :], w_uq_x, w_kn_t, w_v,
      cos_t.reshape(b, s, LANES), sin_t.reshape(b, s, LANES))

    pair = pl.BlockSpec((None, s, 2 * LANES), lambda bi, hp: (bi, 0, hp))
    return pl.pallas_call(
        _mla_attn_kernel,
        grid=(b, MLA_HEADS // 2),
        in_specs=[pair,
                  pl.BlockSpec((None, s, LANES), lambda bi, hp: (bi, 0, hp)),
                  pl.BlockSpec((None, 2 * LANES, s), lambda bi, hp: (bi, hp, 0)),
                  pl.BlockSpec((None, 2 * LANES, s), lambda bi, hp: (bi, 0, 0)),
                  pair],
        out_specs=pair,
        out_shape=jax.ShapeDtypeStruct((b, s, hn), BF16),
        compiler_params=_params("parallel", "parallel"),
        name="mla_attn",
    )(qn, qr, knt, krt, v)


def _zero_at_sequence_start(carry_ref):
    @pl.when(pl.program_id(1) == 0)
    def _():
        carry_ref[...] = jnp.zeros(carry_ref.shape, F32)


def _causal_conv(z, w_ref, col, carry_ref):
    tm, c = z.shape
    k = w_ref.shape[0]
    head = jnp.concatenate([carry_ref[...], z[0:SUBLANES, :]], axis=0)

    def paired(a):
        a1 = pltpu.roll(a, 1, 0)
        total = None
        for j in range(0, k, 2):
            inner = w_ref[k - 1 - j:k - j, col:col + c] * a
            if j + 1 < k:
                inner = inner + w_ref[k - 2 - j:k - 1 - j, col:col + c] * a1
            total = inner if j == 0 else total + pltpu.roll(inner, j, 0)
        return total

    y = paired(z)
    y_head = paired(head)[SUBLANES:, :]
    carry_ref[...] = z[tm - SUBLANES:, :]
    return jnp.concatenate([y_head, y[SUBLANES:, :]], axis=0)


def _sc_kernel(x_ref, g_ref, win_ref, cw_ref, out_ref, buf_ref):
    w = out_ref.shape[-1]
    _zero_at_sequence_start(buf_ref)
    h = _rms(x_ref[...], g_ref[...]).astype(BF16)
    cb = SC_COLS

    def projections(c0):
        return [_dot(h, win_ref[:, part * w + c0:part * w + c0 + cb].astype(BF16)) for part in range(3)]

    nxt = projections(0)
    for c0 in range(0, w, cb):
        zb, zc, zu = nxt
        if c0 + cb < w:
            nxt = projections(c0 + cb)
        y = _causal_conv(zc * zu, cw_ref, c0, buf_ref.at[:, c0:c0 + cb])
        out_ref[:, c0:c0 + cb] = (zb * y).astype(BF16)


def _sc_mixer(x, g, w_in, layer, conv_w):
    b, s, d = x.shape
    w = conv_w.shape[1]
    tm = _tile(s, TM_SC)
    row = lambda c: pl.BlockSpec((None, tm, c), lambda bi, i: (bi, i, 0))
    return pl.pallas_call(
        _sc_kernel,
        grid=(b, s // tm),
        in_specs=[row(d), pl.BlockSpec((1, d), lambda bi, i: (0, 0)),
                  _layer_resident(w_in.shape[1:], layer),
                  pl.BlockSpec(conv_w.shape, lambda bi, i: (0, 0))],
        out_specs=row(w),
        out_shape=jax.ShapeDtypeStruct((b, s, w), BF16),
        scratch_shapes=[pltpu.VMEM((SUBLANES, w), F32)],
        compiler_params=_params("parallel", "arbitrary"),
        name="short_conv",
    )(x, g[None, :], w_in, conv_w)


def _gdn_proj_kernel(x_ref, g_ref, win_ref, cw_ref, alog_ref, dtb_ref,
                     q_ref, k_ref, v_ref, gate_ref, bg_ref, buf_ref):
    hk = GDN_HEADS * GDN_DK
    _zero_at_sequence_start(buf_ref)
    h = _rms(x_ref[...], g_ref[...]).astype(BF16)
    cb = GDN_PROJ_COLS
    proj = lambda c0: _dot(h, win_ref[:, c0:c0 + cb])
    gate_cols = list(range(0, hk, cb))
    z_next = proj(0)
    for c0 in range(0, 3 * hk, cb):
        z = z_next
        if c0 + cb < 3 * hk:
            z_next = proj(c0 + cb)
        if gate_cols:
            g0 = gate_cols.pop(0)
            gate_ref[:, g0:g0 + cb] = proj(3 * hk + g0).astype(BF16)
        part, off = divmod(c0, hk)
        ref = (q_ref, k_ref, v_ref)[part]
        y = _silu(_causal_conv(z, cw_ref, c0, buf_ref.at[:, c0:c0 + cb]))
        if part < 2:
            post = GDN_DK ** -0.5 if part == 0 else 1.0
            for d0 in range(0, cb, GDN_DK):
                yh = y[:, d0:d0 + GDN_DK]
                inv = lax.rsqrt(jnp.sum(yh * yh, axis=-1, keepdims=True) + EPS) * post
                ref[:, off + d0:off + d0 + GDN_DK] = (yh * inv).astype(BF16)
        else:
            ref[:, off:off + cb] = y.astype(BF16)
    for g0 in gate_cols:
        gate_ref[:, g0:g0 + cb] = proj(3 * hk + g0).astype(BF16)
    zb = _dot(h, win_ref[:, 4 * hk:4 * hk + LANES])
    beta = jax.nn.sigmoid(zb)
    t = zb + dtb_ref[...]
    softplus = jnp.maximum(t, 0.0) + jnp.log1p(jnp.exp(-jnp.abs(t)))
    decay = -jnp.exp(alog_ref[...]) * softplus
    lane = lax.broadcasted_iota(jnp.int32, zb.shape, 1)
    bg_ref[...] = jnp.where(lane < GDN_HEADS, beta, decay)


def _cumsum_rows(x):
    n = x.shape[0]
    row = lax.broadcasted_iota(jnp.int32, x.shape, 0)
    shift = 1
    while shift < n:
        x = x + jnp.where(row >= shift, pltpu.roll(x, shift, 0), 0.0)
        shift *= 2
    return x


def _unit_lower_inverses(ms):
    c = ms[0].shape[0]
    eye = (lax.broadcasted_iota(jnp.int32, (c, c), 0) == lax.broadcasted_iota(jnp.int32, (c, c), 1)).astype(F32)
    invs = [eye - m for m in ms]
    powers = [m.astype(BF16) for m in ms]
    width = 2
    while width < c:
        powers = [_dot(p, p).astype(BF16) for p in powers]
        invs = [inv + _dot(inv.astype(BF16), p) for inv, p in zip(invs, powers)]
        width *= 2
    return invs


def _gdn_chunk_kernel(q_ref, k_ref, v_ref, gate_ref, bg_ref, onorm_ref, out_ref, state_ref):
    c = GDN_CHUNK
    nb, t = q_ref.shape[0], q_ref.shape[1]
    nh = GDN_HEADS

    @pl.when(pl.program_id(1) == 0)
    def _():
        state_ref[...] = jnp.zeros(state_ref.shape, F32)

    rows = lax.broadcasted_iota(jnp.int32, (c, c), 0)
    cols = lax.broadcasted_iota(jnp.int32, (c, c), 1)
    tri = rows >= cols
    strict = rows > cols
    onorm = onorm_ref[...]
    sl = [slice(hd * GDN_DK, (hd + 1) * GDN_DK) for hd in range(nh)]

    def intra(chunks):
        items = [(bi, r0, hd) for bi, r0 in chunks for hd in range(nh)]
        bgs = {ch: bg_ref[ch[0], ch[1]:ch[1] + c, :] for ch in chunks}
        gcums = {ch: _cumsum_rows(bgs[ch]) for ch in chunks}
        gcums_t = {ch: gcums[ch].T for ch in chunks}
        egs = {ch: jnp.exp(gcums[ch]) for ch in chunks}
        eds = {ch: jnp.exp(gcums[ch][c - 1:c, :] - gcums[ch]) for ch in chunks}
        column = lambda a, lane: jnp.broadcast_to(a[:, lane:lane + 1], (c, GDN_DK))
        qbf = [q_ref[bi, r0:r0 + c, sl[hd]] for bi, r0, hd in items]
        kbf = [k_ref[bi, r0:r0 + c, sl[hd]] for bi, r0, hd in items]
        vbf = [v_ref[bi, r0:r0 + c, sl[hd]] for bi, r0, hd in items]
        scaled = lambda xs, fs: [(x.astype(F32) * f).astype(BF16) for x, f in zip(xs, fs)]
        beta = [column(bgs[bi, r0], hd) for bi, r0, hd in items]
        eg = [column(egs[bi, r0], nh + hd) for bi, r0, hd in items]
        kb = scaled(kbf, beta)
        vb = scaled(vbf, beta)
        kbe = scaled(kbf, [b * e for b, e in zip(beta, eg)])
        qe = scaled(qbf, eg)
        kd = scaled(kbf, [column(eds[bi, r0], nh + hd) for bi, r0, hd in items])
        gcol = [gcums[bi, r0][:, nh + hd:nh + hd + 1] for bi, r0, hd in items]
        grow = [gcums_t[bi, r0][nh + hd:nh + hd + 1, :] for bi, r0, hd in items]
        decay = [jnp.where(tri, jnp.exp(jnp.where(tri, gc - gr, 0.0)), 0.0) for gc, gr in zip(gcol, grow)]
        kk = [lax.dot_general(a, b, NT_DIMS, preferred_element_type=F32) for a, b in zip(kb, kbf)]
        attn = [(lax.dot_general(a, b, NT_DIMS, preferred_element_type=F32) * d).astype(BF16)
                for a, b, d in zip(qbf, kbf, decay)]
        t_inv = [x.astype(BF16) for x in
                 _unit_lower_inverses([jnp.where(strict, a * d, 0.0) for a, d in zip(kk, decay)])]
        u = [_dot(ti, x) for ti, x in zip(t_inv, vb)]
        w = [_dot(ti, x).astype(BF16) for ti, x in zip(t_inv, kbe)]
        e_last = [egs[bi, r0][c - 1:c, nh + hd:nh + hd + 1] for bi, r0, hd in items]
        fields = (u, w, attn, qe, kd, e_last)
        return {ch: tuple(f[i * nh:(i + 1) * nh] for f in fields) for i, ch in enumerate(chunks)}

    def recur(r0, pres, state):
        u, w, attn, qe, kd, e_last = (sum((list(p[f]) for p in pres), []) for f in range(6))
        sb = [x.astype(BF16) for x in state]
        vnb = [(a - _dot(b, s)).astype(BF16) for a, b, s in zip(u, w, sb)]
        o = [_dot(x, s) + _dot(a, vn) for x, s, a, vn in zip(qe, sb, attn, vnb)]
        state = [s * e + lax.dot_general(x, vn, TN_DIMS, preferred_element_type=F32)
                 for s, e, x, vn in zip(state, e_last, kd, vnb)]
        for bi in range(nb):
            gated = [(_rms(o[bi * nh + hd], onorm)
                      * _silu(gate_ref[bi, r0:r0 + c, sl[hd]].astype(F32))).astype(BF16) for hd in range(nh)]
            out_ref[bi, r0:r0 + c, :] = jnp.concatenate(gated, axis=-1)
        return state

    starts = list(range(0, t, c))
    pre = intra([(bi, r0) for r0 in starts for bi in range(nb)])
    state = [state_ref[i] for i in range(nb * nh)]
    for r0 in starts:
        state = recur(r0, [pre[bi, r0] for bi in range(nb)], state)
    for i in range(nb * nh):
        state_ref[i] = state[i]


def _gdn_mixer(x, g, w_in_x, conv_w, a_log, dt_bias, o_norm):
    b, s, d = x.shape
    hk = GDN_HEADS * GDN_DK
    lanes_pad = jnp.zeros((LANES - 2 * GDN_HEADS,), F32)
    zeros_h = jnp.zeros((GDN_HEADS,), F32)
    alog_row = jnp.concatenate([zeros_h, a_log, lanes_pad])[None, :]
    dtb_row = jnp.concatenate([zeros_h, dt_bias, lanes_pad])[None, :]
    tm = _tile(s, TM_GDN_PROJ)
    row = lambda tt, c: pl.BlockSpec((None, tt, c), lambda bi, i: (bi, i, 0))
    vec = lambda c: pl.BlockSpec((1, c), lambda bi, i: (0, 0))
    q, k, v, gate, bg = pl.pallas_call(
        _gdn_proj_kernel,
        grid=(b, s // tm),
        in_specs=[row(tm, d), vec(d), _resident(w_in_x.shape),
                  pl.BlockSpec(conv_w.shape, lambda bi, i: (0, 0)), vec(LANES), vec(LANES)],
        out_specs=[row(tm, hk), row(tm, hk), row(tm, hk), row(tm, hk), row(tm, LANES)],
        out_shape=[jax.ShapeDtypeStruct((b, s, hk), BF16)] * 4 + [jax.ShapeDtypeStruct((b, s, LANES), F32)],
        scratch_shapes=[pltpu.VMEM((SUBLANES, 3 * hk), F32)],
        compiler_params=_params("parallel", "arbitrary"),
        name="gdn_proj",
    )(x, g[None, :], w_in_x, conv_w, alog_row, dtb_row)

    tt = _tile(s, T_GDN)
    nb = _tile(b, GDN_SEQS)
    seqs = lambda c: pl.BlockSpec((nb, tt, c), lambda bi, i: (bi, i, 0))
    return pl.pallas_call(
        _gdn_chunk_kernel,
        grid=(b // nb, s // tt),
        in_specs=[seqs(hk), seqs(hk), seqs(hk), seqs(hk), seqs(LANES), vec(GDN_DV)],
        out_specs=seqs(hk),
        out_shape=jax.ShapeDtypeStruct((b, s, hk), BF16),
        scratch_shapes=[pltpu.VMEM((nb * GDN_HEADS, GDN_DK, GDN_DV), F32)],
        compiler_params=_params("parallel", "arbitrary"),
        name="gdn_chunk",
    )(q, k, v, gate, bg, o_norm[None, :])


def kernel(x, mem, positions, mla_w_in, mla_q_norm, mla_kv_norm, mla_w_uq, mla_w_ukv, mla_w_o, gdn_w_in, gdn_conv_w, gdn_a_log, gdn_dt_bias, gdn_o_norm, gdn_w_o, sc_w_in, sc_conv_w, sc_w_o, norm_mix, norm_mem, norm_mlp, xa_w_q, xa_w_kv, xa_w_o, mlp_w1, mlp_w2, mem_norm, final_norm):
    b, s, d = x.shape
    depth = norm_mix.shape[0]
    m = mem.shape[1]
    cos_t, sin_t = _rope_tables(positions)
    kv_all = _mem_kv(mem.reshape(b * m, d), mem_norm, xa_w_kv).reshape(depth, b, m, 2 * d)
    for i in range(depth):
        j = i // N_MIXERS
        kind = i % N_MIXERS
        if kind == 0:
            w_in_x, w_uq_x, w_kn_t, w_v = _mla_weights(mla_w_in[j], mla_w_uq[j], mla_w_ukv[j])
            pre = _mla_mixer(x, norm_mix[i], w_in_x, mla_q_norm[j], mla_kv_norm[j], w_uq_x, w_kn_t, w_v,
                             cos_t, sin_t)
            w_mix = mla_w_o
        elif kind == 1:
            pad = jnp.zeros((d, LANES - 2 * GDN_HEADS), F32)
            w_in_x = jnp.concatenate([gdn_w_in[j], pad], axis=1).astype(BF16)
            pre = _gdn_mixer(x, norm_mix[i], w_in_x, gdn_conv_w[j], gdn_a_log[j], gdn_dt_bias[j],
                             gdn_o_norm[j])
            w_mix = gdn_w_o
        else:
            pre = _sc_mixer(x, norm_mix[i], sc_w_in, j, sc_conv_w[j])
            w_mix = sc_w_o
        x = _tail(x, pre, w_mix, j, norm_mem[i], xa_w_q, kv_all, i, xa_w_o)
        x = _mlp(x.reshape(b * s, d), norm_mlp[i], mlp_w1, mlp_w2, i,
                 final_norm, i == depth - 1).reshape(b, s, d)
    return x
```
